```python
import jax, jax.numpy as jnp
from jax import lax
import numpy as np


D_MODEL = 1024
BATCH = 2
SEQ = 16384
DEPTH = 1

A_WIDTH = D_MODEL // 2
A_HEAD = 64
A_HEADS = A_WIDTH // A_HEAD
A_DECAY_RANK = 64
A_ICLR_RANK = 64
A_GATE_RANK = 128
GN_EPS = 64e-5
B_WIDTH = D_MODEL // 2
B_BLOCKS = 8
B_BLOCK = B_WIDTH // B_BLOCKS
CONV_WIDTH = 4
LRU_C = 8.0
IN_COLS = 3 * A_WIDTH + 2 * B_WIDTH + 2 * D_MODEL
N_GROUPS = 4
EXPERTS_PER_GROUP = 8
N_EXPERTS = N_GROUPS * EXPERTS_PER_GROUP
TOP_K = 2
D_EXPERT = 256
EPS = 1e-6

kernel_name = 'hybrid_rwkv7_rglru_hmoe_adaln'


def rmsnorm(x, g):
    xf = x.astype(jnp.float32)
    y = xf * lax.rsqrt(jnp.mean(xf * xf, axis=-1, keepdims=True) + EPS)
    return (y * g.astype(jnp.float32)).astype(x.dtype)


def modulate(h, shift, scale):
    return h * (1 + scale[:, None, :]) + shift[:, None, :]


def token_shift(t):
    return jnp.pad(t, ((0, 0), (1, 0), (0, 0)))[:, :-1]


def rwkv7_time_mix(xn, rkv, mu_rkv, mu_wag, w0, w1, w2, a0, a1, a2, g1, g2,
                   k_k, k_a, r_k, lnx_g, lnx_b):
    bsz, seq, _ = xn.shape
    f32 = jnp.float32
    rkv = rkv + (token_shift(rkv) - rkv) * mu_rkv
    r, k, v = jnp.split(rkv, 3, axis=-1)
    dx = token_shift(xn) - xn
    xw = xn + dx * mu_wag[0]
    xa = xn + dx * mu_wag[1]
    xg = xn + dx * mu_wag[2]
    w_log = -jax.nn.softplus(-(w0 + jnp.tanh(xw @ w1) @ w2)) - 0.5
    decay = jnp.exp(-jnp.exp(w_log.astype(f32)))
    a = jax.nn.sigmoid(a0 + (xa @ a1) @ a2)
    g = jax.nn.sigmoid(xg @ g1) @ g2
    heads = lambda t: t.reshape(bsz, seq, A_HEADS, A_HEAD)
    kk = heads((k * k_k).astype(f32))
    kk = kk * lax.rsqrt(jnp.sum(kk * kk, axis=-1, keepdims=True) + 1e-12)
    k = k * (1 + (a - 1) * k_a)
    r_h, k_h, v_h, a_h = (heads(t.astype(f32)) for t in (r, k, v, a))
    w_h = heads(decay)
    seq_first = lambda t: jnp.moveaxis(t, 1, 0)
    xs = tuple(seq_first(t) for t in (r_h, w_h, k_h, v_h, -kk, kk * a_h))

    def step(state, inp):
        r_t, w_t, k_t, v_t, a_t, b_t = inp
        sa = jnp.einsum('bhvk,bhk->bhv', state, a_t)
        state = (state * w_t[:, :, None, :] + sa[..., None] * b_t[:, :, None, :]
                 + v_t[..., None] * k_t[:, :, None, :])
        y_t = jnp.einsum('bhvk,bhk->bhv', state, r_t)
        return state, y_t

    state0 = jnp.zeros((bsz, A_HEADS, A_HEAD, A_HEAD), f32)
    _, y = lax.scan(step, state0, xs)
    y = jnp.moveaxis(y, 0, 1)
    mean = jnp.mean(y, axis=-1, keepdims=True)
    var = jnp.mean(jnp.square(y - mean), axis=-1, keepdims=True)
    y = ((y - mean) * lax.rsqrt(var + GN_EPS)).reshape(bsz, seq, A_WIDTH)
    y = y * lnx_g.astype(f32) + lnx_b.astype(f32)
    bonus = jnp.sum(r_h * k_h * r_k.astype(f32), axis=-1, keepdims=True) * v_h
    y = y + bonus.reshape(bsz, seq, A_WIDTH)
    return (y * g.astype(f32)).astype(xn.dtype)


def rglru_branch(xb, gb, conv_w, conv_b, w_rgate, b_rgate, w_igate, b_igate, lam):
    bsz, seq, _ = xb.shape
    f32 = jnp.float32
    xc = lax.conv_general_dilated(
        xb, conv_w, window_strides=(1,), padding=((CONV_WIDTH - 1, 0),),
        dimension_numbers=('NWC', 'WIO', 'NWC'), feature_group_count=B_WIDTH) + conv_b
    xblk = xc.reshape(bsz, seq, B_BLOCKS, B_BLOCK)
    gate_r = jax.nn.sigmoid((jnp.einsum('bsgi,gij->bsgj', xblk, w_rgate)
                             .reshape(bsz, seq, B_WIDTH) + b_rgate).astype(f32))
    gate_i = jax.nn.sigmoid((jnp.einsum('bsgi,gij->bsgj', xblk, w_igate)
                             .reshape(bsz, seq, B_WIDTH) + b_igate).astype(f32))
    log_a = -LRU_C * gate_r * jax.nn.softplus(-lam.astype(f32))
    a = jnp.exp(log_a)
    u = jnp.sqrt(-jnp.expm1(2.0 * log_a)) * (gate_i * xc.astype(f32))

    def combine(left, right):
        a_l, u_l = left
        a_r, u_r = right
        return a_l * a_r, a_r * u_l + u_r

    _, h = lax.associative_scan(combine, (a, u), axis=1)
    return (h * jax.nn.gelu(gb.astype(f32))).astype(xb.dtype)


def hier_moe(h, w_rg, b_rg, w_re, b_re, w1e, w3e, w2e):
    bsz, seq, d = h.shape
    f32 = jnp.float32
    t = h.reshape(-1, d)
    pg = jax.nn.softmax((t @ w_rg + b_rg).astype(f32), axis=-1)
    pg_top, g_idx = lax.top_k(pg, 1)
    le = (t @ w_re + b_re).astype(f32).reshape(-1, N_GROUPS, EXPERTS_PER_GROUP)
    le_sel = jnp.take_along_axis(le, g_idx[:, :, None], axis=1)[:, 0]
    pe = jax.nn.softmax(le_sel, axis=-1)
    pe_top, e_idx = lax.top_k(pe, TOP_K)
    pe_top = pe_top / jnp.sum(pe_top, axis=-1, keepdims=True)
    wts = pg_top * pe_top
    flat_idx = g_idx * EXPERTS_PER_GROUP + e_idx
    cw = jnp.sum(jax.nn.one_hot(flat_idx, N_EXPERTS, dtype=f32) * wts[..., None], axis=1)
    cw = cw.astype(h.dtype)
    out = jnp.zeros_like(t)
    for e in range(N_EXPERTS):
        he = jax.nn.silu(t @ w1e[e]) * (t @ w3e[e])
        out = out + cw[:, e:e + 1] * (he @ w2e[e])
    return out.reshape(bsz, seq, d)


def setup_inputs(seed: int = 0) -> dict:
    key = jax.random.key(seed)
    ks = iter(jax.random.split(key, 64))
    nrm = lambda shape, s: jax.random.normal(next(ks), shape, jnp.float32) * s
    uni = lambda shape, lo, hi: jax.random.uniform(next(ks), shape, jnp.float32, lo, hi)
    L, D, A, Bw, F = DEPTH, D_MODEL, A_WIDTH, B_WIDTH, D_EXPERT
    x = nrm((BATCH, SEQ, D), 1.0)
    c = nrm((BATCH, D), 1.0)
    w_ada = nrm((L, D, 6 * D), D ** -0.5)
    b_ada = nrm((L, 6 * D), 0.02)
    g_mix = 1.0 + nrm((L, D), 0.02)
    w_in = nrm((L, D, IN_COLS), D ** -0.5)
    mu_rkv = uni((L, 3 * A), 0.0, 1.0)
    mu_wag = uni((L, 3, D), 0.0, 1.0)
    w0 = uni((L, A), -6.0, -1.0)
    w1 = nrm((L, D, A_DECAY_RANK), D ** -0.5)
    w2 = nrm((L, A_DECAY_RANK, A), 0.5 * A_DECAY_RANK ** -0.5)
    a0 = nrm((L, A), 0.1)
    a1 = nrm((L, D, A_ICLR_RANK), D ** -0.5)
    a2 = nrm((L, A_ICLR_RANK, A), 0.5 * A_ICLR_RANK ** -0.5)
    g1 = nrm((L, D, A_GATE_RANK), D ** -0.5)
    g2 = nrm((L, A_GATE_RANK, A), A_GATE_RANK ** -0.5)
    k_k = 0.85 + nrm((L, A), 0.02)
    k_a = 1.0 + nrm((L, A), 0.02)
    r_k = nrm((L, A_HEADS, A_HEAD), 0.1)
    lnx_g = 1.0 + nrm((L, A), 0.02)
    lnx_b = nrm((L, A), 0.02)
    conv_w = nrm((L, CONV_WIDTH, 1, Bw), CONV_WIDTH ** -0.5)
    conv_b = nrm((L, Bw), 0.02)
    w_rgate = nrm((L, B_BLOCKS, B_BLOCK, B_BLOCK), B_BLOCK ** -0.5)
    b_rgate = nrm((L, Bw), 0.02)
    w_igate = nrm((L, B_BLOCKS, B_BLOCK, B_BLOCK), B_BLOCK ** -0.5)
    b_igate = nrm((L, Bw), 0.02)
    s = uni((L, Bw), 0.9, 0.999) ** (1.0 / LRU_C)
    lam = jnp.log(s) - jnp.log1p(-s)
    p_a = nrm((L, A, D), A ** -0.5)
    p_b = nrm((L, Bw, D), Bw ** -0.5)
    w_out = nrm((L, D, D), D ** -0.5)
    g_ffn = 1.0 + nrm((L, D), 0.02)
    w_rg = nrm((L, D, N_GROUPS), D ** -0.5)
    b_rg = nrm((L, N_GROUPS), 0.01)
    w_re = nrm((L, D, N_EXPERTS), D ** -0.5)
    b_re = nrm((L, N_EXPERTS), 0.01)
    w1e = nrm((L, N_EXPERTS, D, F), D ** -0.5)
    w3e = nrm((L, N_EXPERTS, D, F), D ** -0.5)
    w2e = nrm((L, N_EXPERTS, F, D), F ** -0.5)
    g_final = 1.0 + nrm((D,), 0.02)
    w_ada_f = nrm((D, 2 * D), D ** -0.5)
    b_ada_f = nrm((2 * D,), 0.02)
    return {'x': x, 'c': c, 'w_ada': w_ada, 'b_ada': b_ada, 'g_mix': g_mix, 'w_in': w_in,
            'mu_rkv': mu_rkv, 'mu_wag': mu_wag, 'w0': w0, 'w1': w1, 'w2': w2,
            'a0': a0, 'a1': a1, 'a2': a2, 'g1': g1, 'g2': g2, 'k_k': k_k, 'k_a': k_a,
            'r_k': r_k, 'lnx_g': lnx_g, 'lnx_b': lnx_b, 'conv_w': conv_w, 'conv_b': conv_b,
            'w_rgate': w_rgate, 'b_rgate': b_rgate, 'w_igate': w_igate, 'b_igate': b_igate,
            'lam': lam, 'p_a': p_a, 'p_b': p_b, 'w_out': w_out, 'g_ffn': g_ffn,
            'w_rg': w_rg, 'b_rg': b_rg, 'w_re': w_re, 'b_re': b_re,
            'w1e': w1e, 'w3e': w3e, 'w2e': w2e,
            'g_final': g_final, 'w_ada_f': w_ada_f, 'b_ada_f': b_ada_f}


def reference(x, c, w_ada, b_ada, g_mix, w_in, mu_rkv, mu_wag, w0, w1, w2, a0, a1, a2,
              g1, g2, k_k, k_a, r_k, lnx_g, lnx_b, conv_w, conv_b, w_rgate, b_rgate,
              w_igate, b_igate, lam, p_a, p_b, w_out, g_ffn, w_rg, b_rg, w_re, b_re,
              w1e, w3e, w2e, g_final, w_ada_f, b_ada_f):
    c_act = jax.nn.silu(c)
    o1 = 3 * A_WIDTH
    o2 = o1 + B_WIDTH
    o3 = o2 + B_WIDTH
    o4 = o3 + D_MODEL
    for l in range(DEPTH):
        mod = c_act @ w_ada[l] + b_ada[l]
        sh1, sc1, gt1, sh2, sc2, gt2 = jnp.split(mod, 6, axis=-1)
        h = modulate(rmsnorm(x, g_mix[l]), sh1, sc1)
        proj = h @ w_in[l]
        y_a = rwkv7_time_mix(h, proj[..., :o1], mu_rkv[l], mu_wag[l], w0[l], w1[l], w2[l],
                             a0[l], a1[l], a2[l], g1[l], g2[l], k_k[l], k_a[l], r_k[l],
                             lnx_g[l], lnx_b[l])
        y_b = rglru_branch(proj[..., o1:o2], proj[..., o2:o3], conv_w[l], conv_b[l],
                           w_rgate[l], b_rgate[l], w_igate[l], b_igate[l], lam[l])
        merged = (jax.nn.sigmoid(proj[..., o3:o4]) * (y_a @ p_a[l])
                  + jax.nn.sigmoid(proj[..., o4:]) * (y_b @ p_b[l]))
        x = x + gt1[:, None, :] * (merged @ w_out[l])
        h2 = modulate(rmsnorm(x, g_ffn[l]), sh2, sc2)
        x = x + gt2[:, None, :] * hier_moe(h2, w_rg[l], b_rg[l], w_re[l], b_re[l],
                                           w1e[l], w3e[l], w2e[l])
    shf, scf = jnp.split(c_act @ w_ada_f + b_ada_f, 2, axis=-1)
    return modulate(rmsnorm(x, g_final), shf, scf)
```

```python
import functools

import jax
import jax.numpy as jnp
from jax import lax
from jax.experimental import pallas as pl
from jax.experimental.pallas import tpu as pltpu

F32 = jnp.float32
BF16 = jnp.bfloat16

HEAD = 64
CHUNK = 64
SUB = 16
QUAD = 4
RMS_EPS = 1e-6
GN_EPS = 64e-5
LRU_C = 8.0
CONV_WIDTH = 4
N_GROUPS = 4
EXPERTS_PER_GROUP = 8
VMEM_LIMIT = 56 * 1024 * 1024


def _cparams(n_axes):
    return pltpu.CompilerParams(dimension_semantics=("arbitrary",) * n_axes,
                                vmem_limit_bytes=VMEM_LIMIT)


def _dot(a, b, dims=(((1,), (0,)), ((), ()))):
    return lax.dot_general(a.astype(BF16), b.astype(BF16), dims,
                           preferred_element_type=F32)


NT = (((1,), (1,)), ((), ()))
TN = (((0,), (0,)), ((), ()))


def _split(a):
    hi = a.astype(BF16)
    lo = (a - hi.astype(F32)).astype(BF16)
    return hi, lo


def _dot_hl(a, b_exact, dims=(((1,), (0,)), ((), ()))):
    hi, lo = _split(a)
    return (lax.dot_general(hi, b_exact, dims, preferred_element_type=F32)
            + lax.dot_general(lo, b_exact, dims, preferred_element_type=F32))


def _dot3(a, b, dims=(((1,), (0,)), ((), ()))):
    ah, al = _split(a)
    bh, bl = _split(b)
    d = functools.partial(lax.dot_general, dimension_numbers=dims,
                          preferred_element_type=F32)
    return d(ah, bh) + (d(ah, bl) + d(al, bh))


def _sigmoid(z):
    return 1.0 / (1.0 + jnp.exp(-z))


def _softplus(z):
    return jnp.maximum(z, 0.0) + jnp.log1p(jnp.exp(-jnp.abs(z)))


def _silu(z):
    return z * _sigmoid(z)


def _gelu_tanh(z):
    return 0.5 * z * (1.0 + jnp.tanh(0.7978845608028654 * (z + 0.044715 * (z * z * z))))


def _ada_kernel(c_ref, w_ref, b_ref, o_ref):
    ca = _silu(c_ref[...])
    o_ref[...] = _dot3(ca, w_ref[...]) + b_ref[...]


def _ada(c8, w, b, tn=1024):
    d, n = w.shape
    return pl.pallas_call(
        _ada_kernel,
        grid=(n // tn,),
        in_specs=[pl.BlockSpec((8, d), lambda j: (0, 0)),
                  pl.BlockSpec((d, tn), lambda j: (0, j)),
                  pl.BlockSpec((1, tn), lambda j: (0, j))],
        out_specs=pl.BlockSpec((8, tn), lambda j: (0, j)),
        out_shape=jax.ShapeDtypeStruct((8, n), F32),
        compiler_params=_cparams(1),
        name="adaln",
    )(c8, w, b.reshape(1, n))


def _inproj_kernel(x_ref, sh_ref, sc_ref, g_ref, wext_ref, mu_ref, w0_ref, a0_ref,
                   w2cat_ref, kk_ref, ka_ref, rk_ref, bd_ref,
                   r_out, lw_out, k_out, v_out, kk_out, b_out, g_out, bonus_out,
                   xb_out, gb_out, s1_out, s2_out, carry_ref, *, a_width, d_model):
    A = a_width
    n_shift = 3 * A + 256
    s = pl.program_id(1)

    @pl.when(s == 0)
    def _():
        carry_ref[...] = jnp.zeros_like(carry_ref)

    x = x_ref[0]
    tm = x.shape[0]
    ms = jnp.mean(x * x, axis=-1, keepdims=True)
    h = (x * lax.rsqrt(ms + RMS_EPS)) * g_ref[...]
    h = h * (1.0 + sc_ref[0]) + sh_ref[0]
    hb = h.astype(BF16)

    sg = jnp.dot(hb, wext_ref[:, 0:n_shift + 256], preferred_element_type=F32)
    cur = sg[:, 0:n_shift]
    row = lax.broadcasted_iota(jnp.int32, (tm, 1), 0)
    prev = jnp.where(row == 0, carry_ref[0:1, :], pltpu.roll(cur, 1, axis=0))
    carry_ref[0:1, :] = cur[tm - 1:tm, :]

    rkv = cur[:, 0:3 * A]
    rkv = rkv + (prev[:, 0:3 * A] - rkv) * mu_ref[...]
    r = rkv[:, 0:A]
    k = rkv[:, A:2 * A]
    v = rkv[:, 2 * A:3 * A]

    pre = sg[:, n_shift:n_shift + 256] + prev[:, 3 * A:3 * A + 256]
    lane = lax.broadcasted_iota(jnp.int32, pre.shape, 1)
    act = jnp.where(lane < 64, jnp.tanh(pre), jnp.where(lane < 128, pre, _sigmoid(pre)))
    low = _dot(act, w2cat_ref[...])
    w_log = -_softplus(-(w0_ref[...] + low[:, 0:A])) - 0.5
    lw_out[0] = -jnp.exp(w_log)
    iclr = _sigmoid(a0_ref[...] + low[:, A:2 * A])
    g_out[0] = low[:, 2 * A:3 * A]

    bd = bd_ref[...]
    kkr = k * kk_ref[...]
    kkn = kkr * lax.rsqrt(_dot_hl(kkr * kkr, bd) + 1e-12)
    k2 = k * (1.0 + (iclr - 1.0) * ka_ref[...])
    r_out[0] = r
    k_out[0] = k2
    v_out[0] = v
    kk_out[0] = kkn
    b_out[0] = kkn * iclr
    bonus_out[0] = _dot_hl(r * k2 * rk_ref[...], bd) * v

    o = n_shift + 256
    bw = xb_out.shape[-1]
    xg = jnp.dot(hb, wext_ref[:, o:o + 2 * bw], preferred_element_type=F32)
    xb_out[0] = xg[:, 0:bw]
    gb_out[0] = xg[:, bw:2 * bw]
    o += 2 * bw
    s1_out[0] = _sigmoid(jnp.dot(hb, wext_ref[:, o:o + d_model], preferred_element_type=F32))
    o += d_model
    s2_out[0] = _sigmoid(jnp.dot(hb, wext_ref[:, o:o + d_model], preferred_element_type=F32))


def _inproj(x, sh1, sc1, g_mix, wext, mu_rkv, w0, a0, w2cat, k_k, k_a, r_k, bd, *, tm=256):
    B, S, D = x.shape
    A = w0.shape[-1]
    Bw = A
    ncol = wext.shape[1]
    row = lambda a: a.reshape(1, -1)
    full = lambda a: pl.BlockSpec(a.shape, lambda b, s: (0,) * a.ndim)
    tok = lambda n: pl.BlockSpec((1, tm, n), lambda b, s: (b, s, 0))
    per_b = pl.BlockSpec((1, 1, D), lambda b, s: (b, 0, 0))
    small = [row(g_mix), wext, row(mu_rkv), row(w0), row(a0), w2cat, row(k_k), row(k_a),
             row(r_k), bd]
    outs = [jax.ShapeDtypeStruct((B, S, A), F32)] * 8 + \
           [jax.ShapeDtypeStruct((B, S, Bw), F32)] * 2 + \
           [jax.ShapeDtypeStruct((B, S, D), F32)] * 2
    return pl.pallas_call(
        functools.partial(_inproj_kernel, a_width=A, d_model=D),
        grid=(B, S // tm),
        in_specs=[tok(D), per_b, per_b] + [full(a) for a in small],
        out_specs=[tok(A)] * 8 + [tok(Bw)] * 2 + [tok(D)] * 2,
        out_shape=outs,
        scratch_shapes=[pltpu.VMEM((8, 3 * A + 256), F32)],
        compiler_params=_cparams(2),
        name="inproj",
    )(x, sh1.reshape(B, 1, D), sc1.reshape(B, 1, D), *small)


def _rwkv_kernel(r_ref, lw_ref, k_ref, v_ref, kk_ref, b_ref, tri_ref, bdm_ref, slm_ref,
                 lm_ref, dgm_ref, y_ref, s_ref):
    c = pl.program_id(2)

    @pl.when(c == 0)
    def _():
        s_ref[...] = jnp.zeros_like(s_ref)

    L = CHUNK
    r = r_ref[0]
    lw = lw_ref[0]
    k = k_ref[0]
    v = v_ref[0]
    a = -kk_ref[0]
    b = b_ref[0]
    bdm = bdm_ref[...]
    slm = slm_ref[...]
    lm = lm_ref[...]

    cl = _dot_hl3(tri_ref[...], lw)
    clp = cl - lw
    cm = cl[L // 2 - 1:L // 2, :]
    ce = cl[L - 1:L, :]
    at = a * jnp.exp(clp - cm)
    rt = r * jnp.exp(cl - cm)
    e_inv = jnp.exp(cm - cl)
    bt = b * e_inv
    kt = k * e_inv
    e_end = jnp.exp(ce - cl)
    bh = b * e_end
    kh = k * e_end
    g_mid = jnp.exp(cm)
    g_end = jnp.exp(ce)

    tile = lambda m: jnp.concatenate([m] * QUAD, axis=0)
    stack = lambda m: tile(m) * bdm
    a_s = stack(at)
    r_s = stack(rt)
    v_s = stack(v)
    bh_s = stack(bh)
    kh_s = stack(kh)
    bt_t = tile(bt)
    kt_t = tile(kt)

    a_ab = _dot(a_s, bt_t, NT) * slm
    a_ak = _dot(a_s, kt_t, NT) * slm
    a_rb = _dot(r_s, bt_t, NT) * lm
    a_rk = _dot(r_s, kt_t, NT) * lm

    d = a_ab * dgm_ref[...]
    e = a_ab - d
    p = d
    d_inv = (lm - slm) + p
    n = 2
    while n < SUB:
        p = _dot(p, p)
        d_inv = d_inv + _dot(d_inv, p)
        n *= 2
    f = _dot(d_inv, e)
    t_inv = d_inv
    fp = f
    n = 1
    terms = []
    while n < L // SUB:
        terms.append(fp)
        n *= 2
        if n < L // SUB:
            fp = _dot(fp, fp)
    for fp in reversed(terms):
        t_inv = t_inv + _dot(fp, t_inv)

    w_s = _dot(t_inv, a_s)
    u0 = _dot(t_inv, _dot(a_ak, v_s))
    q_s = r_s + _dot(a_rb, w_s)
    y0 = _dot(a_rb, u0) + _dot(a_rk, v_s)

    s0 = s_ref[...]
    s_mid = s0 * g_mid
    y_s = _dot(q_s, s_mid, NT) + y0
    g_t = _dot(w_s, bh_s, TN)
    h_t = _dot(u0, bh_s, TN) + _dot(v_s, kh_s, TN)
    s_ref[...] = s0 * g_end + _dot(s_mid, g_t) + h_t

    y = y_s[0:L]
    for q in range(1, QUAD):
        y = y + y_s[q * L:(q + 1) * L]
    y_ref[0] = y


def _dot_hl3(tri, lw):
    hi = lw.astype(BF16)
    r1 = lw - hi.astype(F32)
    mid = r1.astype(BF16)
    lo = (r1 - mid.astype(F32)).astype(BF16)
    d = functools.partial(jnp.dot, preferred_element_type=F32)
    return d(tri, hi) + (d(tri, mid) + d(tri, lo))


def _rwkv(r, lw, k, v, kk, b):
    B, S, A = r.shape
    W = QUAD * HEAD
    L = CHUNK
    idx = jnp.arange(W)
    assert L == HEAD, "stacked rows (head, time) and lanes (head, channel) share one block size"
    same = (idx[:, None] // L) == (idx[None, :] // L)
    ti = idx[:, None] % L
    tj = idx[None, :] % L
    bdm = same.astype(F32)
    slm = (same & (ti > tj)).astype(F32)
    lm = (same & (ti >= tj)).astype(F32)
    dgm = (same & (ti > tj) & (ti // SUB == tj // SUB)).astype(F32)
    tri = (jnp.arange(L)[:, None] >= jnp.arange(L)[None, :]).astype(BF16)
    tok = pl.BlockSpec((1, L, W), lambda bb, q, c: (bb, c, q))
    full = lambda a: pl.BlockSpec(a.shape, lambda bb, q, c: (0,) * a.ndim)
    return pl.pallas_call(
        _rwkv_kernel,
        grid=(B, A // W, S // L),
        in_specs=[tok] * 6 + [full(tri), full(bdm), full(slm), full(lm), full(dgm)],
        out_specs=tok,
        out_shape=jax.ShapeDtypeStruct((B, S, A), F32),
        scratch_shapes=[pltpu.VMEM((W, W), F32)],
        compiler_params=_cparams(3),
        name="rwkv7",
    )(r, lw, k, v, kk, b, tri, bdm, slm, lm, dgm)


def _rglru_kernel(xb_ref, gb_ref, cw_ref, cb_ref, wr_ref, br_ref, wi_ref, bi_ref, lam_ref,
                  y_ref, xc_carry, h_carry):
    s = pl.program_id(1)

    @pl.when(s == 0)
    def _():
        xc_carry[...] = jnp.zeros_like(xc_carry)
        h_carry[...] = jnp.zeros_like(h_carry)

    xb = xb_ref[0]
    tt = xb.shape[0]
    ext = jnp.concatenate([xc_carry[...], xb], axis=0)
    xc_carry[...] = xb[tt - 8:tt, :]
    xc = cb_ref[...] + cw_ref[CONV_WIDTH - 1:CONV_WIDTH, :] * xb
    for d in range(1, CONV_WIDTH):
        xc = xc + cw_ref[CONV_WIDTH - 1 - d:CONV_WIDTH - d, :] * ext[8 - d:8 - d + tt, :]

    gate_r = _sigmoid(_dot(xc, wr_ref[...]) + br_ref[...])
    gate_i = _sigmoid(_dot(xc, wi_ref[...]) + bi_ref[...])
    log_a = -LRU_C * gate_r * _softplus(-lam_ref[...])
    a = jnp.exp(log_a)
    u = jnp.sqrt(-jnp.tanh(log_a) * (a * a + 1.0)) * (gate_i * xc)

    row = lax.broadcasted_iota(jnp.int32, (tt, 1), 0)
    d = 1
    while d < tt:
        keep = row >= d
        a_sh = jnp.where(keep, pltpu.roll(a, d, axis=0), 1.0)
        u_sh = jnp.where(keep, pltpu.roll(u, d, axis=0), 0.0)
        u = a * u_sh + u
        a = a * a_sh
        d *= 2
    h = a * h_carry[0:1, :] + u
    h_carry[0:1, :] = h[tt - 1:tt, :]
    y_ref[0] = h * _gelu_tanh(gb_ref[0])


def _rglru(xb, gb, conv_w, conv_b, wr, br, wi, bi, lam, *, tt=256):
    B, S, W = xb.shape
    row = lambda a: a.reshape(1, -1)
    full = lambda a: pl.BlockSpec(a.shape, lambda b, s: (0,) * a.ndim)
    tok = pl.BlockSpec((1, tt, W), lambda b, s: (b, s, 0))
    small = [conv_w, row(conv_b), wr, row(br), wi, row(bi), row(lam)]
    return pl.pallas_call(
        _rglru_kernel,
        grid=(B, S // tt),
        in_specs=[tok, tok] + [full(a) for a in small],
        out_specs=tok,
        out_shape=jax.ShapeDtypeStruct((B, S, W), F32),
        scratch_shapes=[pltpu.VMEM((8, W), F32), pltpu.VMEM((8, W), F32)],
        compiler_params=_cparams(2),
        name="rglru",
    )(xb, gb, *small)


def _merge_kernel(x_ref, ya_ref, bonus_ref, g_ref, yb_ref, s1_ref, s2_ref, gt1_ref, sh2_ref,
                  sc2_ref, lng_ref, lnb_ref, bd_ref, pa_ref, pb_ref, wout_ref, gffn_ref,
                  wr_ref, br_ref, x1_out, h2_out, cw_out):
    ya = ya_ref[0]
    bd = bd_ref[...]
    inv_n = 1.0 / HEAD
    mean = _dot_hl(ya, bd) * inv_n
    yc = ya - mean
    var = _dot_hl(yc * yc, bd) * inv_n
    yn = yc * lax.rsqrt(var + GN_EPS) * lng_ref[...] + lnb_ref[...]
    ya2 = (yn + bonus_ref[0]) * g_ref[0]

    merged = s1_ref[0] * _dot(ya2, pa_ref[...]) + s2_ref[0] * _dot(yb_ref[0], pb_ref[...])
    x1 = x_ref[0] + gt1_ref[0] * _dot(merged, wout_ref[...])
    x1_out[0] = x1

    ms = jnp.mean(x1 * x1, axis=-1, keepdims=True)
    h2 = (x1 * lax.rsqrt(ms + RMS_EPS)) * gffn_ref[...]
    h2 = h2 * (1.0 + sc2_ref[0]) + sh2_ref[0]
    h2_out[0] = h2.astype(BF16)

    logits = _dot3(h2, wr_ref[...]) + br_ref[...]
    lane = lax.broadcasted_iota(jnp.int32, logits.shape, 1).astype(F32)
    neg = jnp.float32(-jnp.inf)
    big = jnp.float32(1e9)
    is_g = lane < N_GROUPS
    lg = jnp.where(is_g, logits, neg)
    mg = jnp.max(lg, axis=-1, keepdims=True)
    pg_top = 1.0 / jnp.sum(jnp.exp(lg - mg), axis=-1, keepdims=True)
    g_idx = jnp.min(jnp.where(lg == mg, lane, big), axis=-1, keepdims=True)
    e_lane = lane - N_GROUPS
    in_grp = (e_lane >= g_idx * EXPERTS_PER_GROUP) & (e_lane < (g_idx + 1) * EXPERTS_PER_GROUP)
    le = jnp.where(in_grp, logits, neg)
    me = jnp.max(le, axis=-1, keepdims=True)
    ee = jnp.exp(le - me)
    pe = ee / jnp.sum(ee, axis=-1, keepdims=True)
    p1 = jnp.max(pe, axis=-1, keepdims=True)
    i1 = jnp.min(jnp.where(in_grp & (pe == p1), lane, big), axis=-1, keepdims=True)
    rest = jnp.where(in_grp & (lane != i1), pe, -1.0)
    p2 = jnp.max(rest, axis=-1, keepdims=True)
    i2 = jnp.min(jnp.where(rest == p2, lane, big), axis=-1, keepdims=True)
    den = p1 + p2
    cw = jnp.where(lane == i1, pg_top * (p1 / den), 0.0) + \
        jnp.where(lane == i2, pg_top * (p2 / den), 0.0)
    cw_out[0] = cw


def _merge(x, ya, bonus, g, yb, s1, s2, gt1, sh2, sc2, lnx_g, lnx_b, bd, pa, pb, wout, g_ffn,
           wr, br, *, tm=256):
    B, S, D = x.shape
    A = ya.shape[-1]
    row = lambda a: a.reshape(1, -1)
    full = lambda a: pl.BlockSpec(a.shape, lambda b, s: (0,) * a.ndim)
    tok = lambda n: pl.BlockSpec((1, tm, n), lambda b, s: (b, s, 0))
    per_b = pl.BlockSpec((1, 1, D), lambda b, s: (b, 0, 0))
    small = [row(lnx_g), row(lnx_b), bd, pa, pb, wout, row(g_ffn), wr, row(br)]
    return pl.pallas_call(
        _merge_kernel,
        grid=(B, S // tm),
        in_specs=[tok(D), tok(A), tok(A), tok(A), tok(A), tok(D), tok(D), per_b, per_b, per_b]
        + [full(a) for a in small],
        out_specs=[tok(D), tok(D), tok(128)],
        out_shape=[jax.ShapeDtypeStruct((B, S, D), F32),
                   jax.ShapeDtypeStruct((B, S, D), BF16),
                   jax.ShapeDtypeStruct((B, S, 128), F32)],
        compiler_params=_cparams(2),
        name="merge_router",
    )(x, ya, bonus, g, yb, s1, s2, gt1.reshape(B, 1, D), sh2.reshape(B, 1, D),
      sc2.reshape(B, 1, D), *small)


def _moe_kernel(x1_ref, h2_ref, cw_ref, w1_ref, w3_ref, w2_ref, gt2_ref, gf_ref, shf_ref,
                scf_ref, o_ref, acc_ref):
    e = pl.program_id(2)
    n_e = pl.num_programs(2)

    @pl.when(e == 0)
    def _():
        acc_ref[...] = jnp.zeros_like(acc_ref)

    h2 = h2_ref[0]
    he = _silu(jnp.dot(h2, w1_ref[0], preferred_element_type=F32)) * \
        jnp.dot(h2, w3_ref[0], preferred_element_type=F32)
    oe = _dot(he, w2_ref[0])
    cw = cw_ref[0]
    lane = lax.broadcasted_iota(jnp.int32, cw.shape, 1)
    col = jnp.sum(jnp.where(lane == e + N_GROUPS, cw, 0.0), axis=-1, keepdims=True)
    acc_ref[...] += col * oe

    @pl.when(e == n_e - 1)
    def _():
        x2 = x1_ref[0] + gt2_ref[0] * acc_ref[...]
        ms = jnp.mean(x2 * x2, axis=-1, keepdims=True)
        y = (x2 * lax.rsqrt(ms + RMS_EPS)) * gf_ref[...]
        o_ref[0] = y * (1.0 + scf_ref[0]) + shf_ref[0]


def _moe(x1, h2, cw, w1, w3, w2, gt2, g_final, shf, scf, *, tm=512):
    B, S, D = x1.shape
    E, _, F = w1.shape
    tok = lambda n: pl.BlockSpec((1, tm, n), lambda b, s, e: (b, s, 0))
    per_b = pl.BlockSpec((1, 1, D), lambda b, s, e: (b, 0, 0))
    return pl.pallas_call(
        _moe_kernel,
        grid=(B, S // tm, E),
        in_specs=[tok(D), tok(D), tok(128),
                  pl.BlockSpec((1, D, F), lambda b, s, e: (e, 0, 0)),
                  pl.BlockSpec((1, D, F), lambda b, s, e: (e, 0, 0)),
                  pl.BlockSpec((1, F, D), lambda b, s, e: (e, 0, 0)),
                  per_b, pl.BlockSpec((1, D), lambda b, s, e: (0, 0)), per_b, per_b],
        out_specs=tok(D),
        out_shape=jax.ShapeDtypeStruct((B, S, D), F32),
        scratch_shapes=[pltpu.VMEM((tm, D), F32)],
        compiler_params=_cparams(3),
        name="moe",
    )(x1, h2, cw, w1, w3, w2, gt2.reshape(B, 1, D), g_final.reshape(1, D),
      shf.reshape(B, 1, D), scf.reshape(B, 1, D))


def _block_diag(w):
    G, n, _ = w.shape
    eye = jnp.eye(G, dtype=w.dtype)
    return (eye[:, None, :, None] * w[:, :, None, :]).reshape(G * n, G * n)


def kernel(x, c, w_ada, b_ada, g_mix, w_in, mu_rkv, mu_wag, w0, w1, w2, a0, a1, a2, g1, g2, k_k, k_a, r_k, lnx_g, lnx_b, conv_w, conv_b, w_rgate, b_rgate, w_igate, b_igate, lam, p_a, p_b, w_out, g_ffn, w_rg, b_rg, w_re, b_re, w1e, w3e, w2e, g_final, w_ada_f, b_ada_f):
    B, S, D = x.shape
    depth = w_ada.shape[0]
    assert depth == 1, "the final norm is fused into the last MoE call; one layer supported"
    A = w0.shape[-1]
    Bw = lam.shape[-1]

    c8 = jnp.zeros((8, D), F32).at[:B].set(c)
    modf = _ada(c8, w_ada_f, b_ada_f)[:B]
    shf, scf = modf[:, :D], modf[:, D:]

    head_ids = jnp.arange(A) // HEAD
    bd = (head_ids[:, None] == head_ids[None, :]).astype(BF16)

    for l in range(depth):
        mod = _ada(c8, w_ada[l], b_ada[l])[:B]
        sh1, sc1, gt1, sh2, sc2, gt2 = (mod[:, i * D:(i + 1) * D] for i in range(6))

        lowrank = jnp.concatenate([w1[l], a1[l], g1[l]], axis=1)
        mu_cols = jnp.concatenate([jnp.broadcast_to(mu_wag[l, 0][:, None], w1[l].shape),
                                   jnp.broadcast_to(mu_wag[l, 1][:, None], a1[l].shape),
                                   jnp.broadcast_to(mu_wag[l, 2][:, None], g1[l].shape)], axis=1)
        o1 = 3 * A
        wext = jnp.concatenate([w_in[l][:, :o1], mu_cols * lowrank, (1.0 - mu_cols) * lowrank,
                                w_in[l][:, o1:]], axis=1).astype(BF16)
        ra, rb = w2.shape[1], g2.shape[1]
        w2cat = jnp.zeros((2 * ra + rb, 3 * A), F32)
        w2cat = w2cat.at[0:ra, 0:A].set(w2[l]).at[ra:2 * ra, A:2 * A].set(a2[l])
        w2cat = w2cat.at[2 * ra:, 2 * A:].set(g2[l]).astype(BF16)

        (r, lw, k, v, kk, bvec, g, bonus, xb, gb, s1, s2) = _inproj(
            x, sh1, sc1, g_mix[l], wext, mu_rkv[l], w0[l], a0[l], w2cat, k_k[l], k_a[l],
            r_k[l].reshape(-1), bd)

        ya = _rwkv(r, lw, k, v, kk, bvec)
        yb = _rglru(xb, gb, conv_w[l].reshape(CONV_WIDTH, Bw), conv_b[l],
                    _block_diag(w_rgate[l]).astype(BF16), b_rgate[l],
                    _block_diag(w_igate[l]).astype(BF16), b_igate[l], lam[l])

        n_g, n_e = w_rg.shape[-1], w_re.shape[-1]
        wr = jnp.zeros((D, 128), F32).at[:, :n_g].set(w_rg[l]).at[:, n_g:n_g + n_e].set(w_re[l])
        br = jnp.zeros((128,), F32).at[:n_g].set(b_rg[l]).at[n_g:n_g + n_e].set(b_re[l])
        x1, h2, cw = _merge(x, ya, bonus, g, yb, s1, s2, gt1, sh2, sc2, lnx_g[l], lnx_b[l], bd,
                            p_a[l].astype(BF16), p_b[l].astype(BF16), w_out[l].astype(BF16),
                            g_ffn[l], wr, br)
        x = _moe(x1, h2, cw, w1e[l].astype(BF16), w3e[l].astype(BF16), w2e[l].astype(BF16),
                 gt2, g_final, shf, scf)
    return x
```

```python
import functools

import jax
import jax.numpy as jnp
from jax import lax
from jax.experimental import pallas as pl
from jax.experimental.pallas import tpu as pltpu

F32 = jnp.float32
BF16 = jnp.bfloat16

HEAD = 64
CHUNK = 64
SUB = 16
QUAD = 4
RMS_EPS = 1e-6
GN_EPS = 64e-5
LRU_C = 8.0
CONV_WIDTH = 4
N_GROUPS = 4
EXPERTS_PER_GROUP = 8
MOE_TILE = 1024
MOE_ROWS = 128
VMEM_LIMIT = 56 * 1024 * 1024


def _cparams(n_axes):
    return pltpu.CompilerParams(dimension_semantics=("arbitrary",) * n_axes,
                                vmem_limit_bytes=VMEM_LIMIT)


def _dot(a, b, dims=(((1,), (0,)), ((), ()))):
    return lax.dot_general(a.astype(BF16), b.astype(BF16), dims,
                           preferred_element_type=F32)


NT = (((1,), (1,)), ((), ()))
TN = (((0,), (0,)), ((), ()))


def _split(a):
    hi = a.astype(BF16)
    lo = (a - hi.astype(F32)).astype(BF16)
    return hi, lo


def _dot_hl(a, b_exact, dims=(((1,), (0,)), ((), ()))):
    hi, lo = _split(a)
    return (lax.dot_general(hi, b_exact, dims, preferred_element_type=F32)
            + lax.dot_general(lo, b_exact, dims, preferred_element_type=F32))


def _dot3(a, b, dims=(((1,), (0,)), ((), ()))):
    ah, al = _split(a)
    bh, bl = _split(b)
    d = functools.partial(lax.dot_general, dimension_numbers=dims,
                          preferred_element_type=F32)
    return d(ah, bh) + (d(ah, bl) + d(al, bh))


def _sigmoid(z):
    return 1.0 / (1.0 + jnp.exp(-z))


def _softplus(z):
    return jnp.maximum(z, 0.0) + jnp.log1p(jnp.exp(-jnp.abs(z)))


def _silu(z):
    return z * _sigmoid(z)


def _gelu_tanh(z):
    return 0.5 * z * (1.0 + jnp.tanh(0.7978845608028654 * (z + 0.044715 * (z * z * z))))


def _ada_kernel(c_ref, w_ref, b_ref, o_ref):
    ca = _silu(c_ref[...])
    o_ref[...] = _dot3(ca, w_ref[...]) + b_ref[...]


def _ada(c8, w, b, tn=1024):
    d, n = w.shape
    return pl.pallas_call(
        _ada_kernel,
        grid=(n // tn,),
        in_specs=[pl.BlockSpec((8, d), lambda j: (0, 0)),
                  pl.BlockSpec((d, tn), lambda j: (0, j)),
                  pl.BlockSpec((1, tn), lambda j: (0, j))],
        out_specs=pl.BlockSpec((8, tn), lambda j: (0, j)),
        out_shape=jax.ShapeDtypeStruct((8, n), F32),
        compiler_params=_cparams(1),
        name="adaln",
    )(c8, w, b.reshape(1, n))


def _inproj_kernel(x_ref, sh_ref, sc_ref, g_ref, wext_ref, mu_ref, w0_ref, a0_ref,
                   w2cat_ref, kk_ref, ka_ref, rk_ref, bd_ref,
                   r_out, lw_out, k_out, v_out, kk_out, b_out, g_out, bonus_out,
                   xb_out, gb_out, s1_out, s2_out, carry_ref, *, a_width, d_model):
    A = a_width
    n_shift = 3 * A + 256
    s = pl.program_id(1)

    @pl.when(s == 0)
    def _():
        carry_ref[...] = jnp.zeros_like(carry_ref)

    x = x_ref[0]
    tm = x.shape[0]
    ms = jnp.mean(x * x, axis=-1, keepdims=True)
    h = (x * lax.rsqrt(ms + RMS_EPS)) * g_ref[...]
    h = h * (1.0 + sc_ref[0]) + sh_ref[0]
    hb = h.astype(BF16)

    sg = jnp.dot(hb, wext_ref[:, 0:n_shift + 256], preferred_element_type=F32)
    cur = sg[:, 0:n_shift]
    row = lax.broadcasted_iota(jnp.int32, (tm, 1), 0)
    prev = jnp.where(row == 0, carry_ref[0:1, :], pltpu.roll(cur, 1, axis=0))
    carry_ref[0:1, :] = cur[tm - 1:tm, :]

    rkv = cur[:, 0:3 * A]
    rkv = rkv + (prev[:, 0:3 * A] - rkv) * mu_ref[...]
    r = rkv[:, 0:A]
    k = rkv[:, A:2 * A]
    v = rkv[:, 2 * A:3 * A]

    pre = sg[:, n_shift:n_shift + 256] + prev[:, 3 * A:3 * A + 256]
    lane = lax.broadcasted_iota(jnp.int32, pre.shape, 1)
    act = jnp.where(lane < 64, jnp.tanh(pre), jnp.where(lane < 128, pre, _sigmoid(pre)))
    low = _dot(act, w2cat_ref[...])
    w_log = -_softplus(-(w0_ref[...] + low[:, 0:A])) - 0.5
    lw_out[0] = -jnp.exp(w_log)
    iclr = _sigmoid(a0_ref[...] + low[:, A:2 * A])
    g_out[0] = low[:, 2 * A:3 * A]

    bd = bd_ref[...]
    kkr = k * kk_ref[...]
    kkn = kkr * lax.rsqrt(_dot_hl(kkr * kkr, bd) + 1e-12)
    k2 = k * (1.0 + (iclr - 1.0) * ka_ref[...])
    r_out[0] = r
    k_out[0] = k2
    v_out[0] = v
    kk_out[0] = kkn
    b_out[0] = kkn * iclr
    bonus_out[0] = _dot_hl(r * k2 * rk_ref[...], bd) * v

    o = n_shift + 256
    bw = xb_out.shape[-1]
    xg = jnp.dot(hb, wext_ref[:, o:o + 2 * bw], preferred_element_type=F32)
    xb_out[0] = xg[:, 0:bw]
    gb_out[0] = xg[:, bw:2 * bw]
    o += 2 * bw
    s1_out[0] = _sigmoid(jnp.dot(hb, wext_ref[:, o:o + d_model], preferred_element_type=F32))
    o += d_model
    s2_out[0] = _sigmoid(jnp.dot(hb, wext_ref[:, o:o + d_model], preferred_element_type=F32))


def _inproj(x, sh1, sc1, g_mix, wext, mu_rkv, w0, a0, w2cat, k_k, k_a, r_k, bd, *, tm=256):
    B, S, D = x.shape
    A = w0.shape[-1]
    Bw = A
    ncol = wext.shape[1]
    row = lambda a: a.reshape(1, -1)
    full = lambda a: pl.BlockSpec(a.shape, lambda b, s: (0,) * a.ndim)
    tok = lambda n: pl.BlockSpec((1, tm, n), lambda b, s: (b, s, 0))
    per_b = pl.BlockSpec((1, 1, D), lambda b, s: (b, 0, 0))
    small = [row(g_mix), wext, row(mu_rkv), row(w0), row(a0), w2cat, row(k_k), row(k_a),
             row(r_k), bd]
    outs = [jax.ShapeDtypeStruct((B, S, A), F32)] * 8 + \
           [jax.ShapeDtypeStruct((B, S, Bw), F32)] * 2 + \
           [jax.ShapeDtypeStruct((B, S, D), F32)] * 2
    return pl.pallas_call(
        functools.partial(_inproj_kernel, a_width=A, d_model=D),
        grid=(B, S // tm),
        in_specs=[tok(D), per_b, per_b] + [full(a) for a in small],
        out_specs=[tok(A)] * 8 + [tok(Bw)] * 2 + [tok(D)] * 2,
        out_shape=outs,
        scratch_shapes=[pltpu.VMEM((8, 3 * A + 256), F32)],
        compiler_params=_cparams(2),
        name="inproj",
    )(x, sh1.reshape(B, 1, D), sc1.reshape(B, 1, D), *small)


def _rwkv_kernel(r_ref, lw_ref, k_ref, v_ref, kk_ref, b_ref, tri_ref, bdm_ref, slm_ref,
                 lm_ref, dgm_ref, y_ref, s_ref):
    c = pl.program_id(2)

    @pl.when(c == 0)
    def _():
        s_ref[...] = jnp.zeros_like(s_ref)

    L = CHUNK
    r = r_ref[0]
    lw = lw_ref[0]
    k = k_ref[0]
    v = v_ref[0]
    a = -kk_ref[0]
    b = b_ref[0]
    bdm = bdm_ref[...]
    slm = slm_ref[...]
    lm = lm_ref[...]

    cl = _dot_hl3(tri_ref[...], lw)
    clp = cl - lw
    cm = cl[L // 2 - 1:L // 2, :]
    ce = cl[L - 1:L, :]
    at = a * jnp.exp(clp - cm)
    rt = r * jnp.exp(cl - cm)
    e_inv = jnp.exp(cm - cl)
    bt = b * e_inv
    kt = k * e_inv
    e_end = jnp.exp(ce - cl)
    bh = b * e_end
    kh = k * e_end
    g_mid = jnp.exp(cm)
    g_end = jnp.exp(ce)

    tile = lambda m: jnp.concatenate([m] * QUAD, axis=0)
    stack = lambda m: tile(m) * bdm
    a_s = stack(at)
    r_s = stack(rt)
    v_s = stack(v)
    bh_s = stack(bh)
    kh_s = stack(kh)
    bt_t = tile(bt)
    kt_t = tile(kt)

    a_ab = _dot(a_s, bt_t, NT) * slm
    a_ak = _dot(a_s, kt_t, NT) * slm
    a_rb = _dot(r_s, bt_t, NT) * lm
    a_rk = _dot(r_s, kt_t, NT) * lm

    d = a_ab * dgm_ref[...]
    e = a_ab - d
    p = d
    d_inv = (lm - slm) + p
    n = 2
    while n < SUB:
        p = _dot(p, p)
        d_inv = d_inv + _dot(d_inv, p)
        n *= 2
    f = _dot(d_inv, e)
    t_inv = d_inv
    fp = f
    n = 1
    terms = []
    while n < L // SUB:
        terms.append(fp)
        n *= 2
        if n < L // SUB:
            fp = _dot(fp, fp)
    for fp in reversed(terms):
        t_inv = t_inv + _dot(fp, t_inv)

    w_s = _dot(t_inv, a_s)
    u0 = _dot(t_inv, _dot(a_ak, v_s))
    q_s = r_s + _dot(a_rb, w_s)
    y0 = _dot(a_rb, u0) + _dot(a_rk, v_s)

    s0 = s_ref[...]
    s_mid = s0 * g_mid
    y_s = _dot(q_s, s_mid, NT) + y0
    g_t = _dot(w_s, bh_s, TN)
    h_t = _dot(u0, bh_s, TN) + _dot(v_s, kh_s, TN)
    s_ref[...] = s0 * g_end + _dot(s_mid, g_t) + h_t

    y = y_s[0:L]
    for q in range(1, QUAD):
        y = y + y_s[q * L:(q + 1) * L]
    y_ref[0] = y


def _dot_hl3(tri, lw):
    hi = lw.astype(BF16)
    r1 = lw - hi.astype(F32)
    mid = r1.astype(BF16)
    lo = (r1 - mid.astype(F32)).astype(BF16)
    d = functools.partial(jnp.dot, preferred_element_type=F32)
    return d(tri, hi) + (d(tri, mid) + d(tri, lo))


def _rwkv(r, lw, k, v, kk, b):
    B, S, A = r.shape
    W = QUAD * HEAD
    L = CHUNK
    idx = jnp.arange(W)
    assert L == HEAD, "stacked rows (head, time) and lanes (head, channel) share one block size"
    same = (idx[:, None] // L) == (idx[None, :] // L)
    ti = idx[:, None] % L
    tj = idx[None, :] % L
    bdm = same.astype(F32)
    slm = (same & (ti > tj)).astype(F32)
    lm = (same & (ti >= tj)).astype(F32)
    dgm = (same & (ti > tj) & (ti // SUB == tj // SUB)).astype(F32)
    tri = (jnp.arange(L)[:, None] >= jnp.arange(L)[None, :]).astype(BF16)
    tok = pl.BlockSpec((1, L, W), lambda bb, q, c: (bb, c, q))
    full = lambda a: pl.BlockSpec(a.shape, lambda bb, q, c: (0,) * a.ndim)
    return pl.pallas_call(
        _rwkv_kernel,
        grid=(B, A // W, S // L),
        in_specs=[tok] * 6 + [full(tri), full(bdm), full(slm), full(lm), full(dgm)],
        out_specs=tok,
        out_shape=jax.ShapeDtypeStruct((B, S, A), F32),
        scratch_shapes=[pltpu.VMEM((W, W), F32)],
        compiler_params=_cparams(3),
        name="rwkv7",
    )(r, lw, k, v, kk, b, tri, bdm, slm, lm, dgm)


def _rglru_kernel(xb_ref, gb_ref, cw_ref, cb_ref, wr_ref, br_ref, wi_ref, bi_ref, lam_ref,
                  y_ref, xc_carry, h_carry):
    s = pl.program_id(1)

    @pl.when(s == 0)
    def _():
        xc_carry[...] = jnp.zeros_like(xc_carry)
        h_carry[...] = jnp.zeros_like(h_carry)

    xb = xb_ref[0]
    tt = xb.shape[0]
    ext = jnp.concatenate([xc_carry[...], xb], axis=0)
    xc_carry[...] = xb[tt - 8:tt, :]
    xc = cb_ref[...] + cw_ref[CONV_WIDTH - 1:CONV_WIDTH, :] * xb
    for d in range(1, CONV_WIDTH):
        xc = xc + cw_ref[CONV_WIDTH - 1 - d:CONV_WIDTH - d, :] * ext[8 - d:8 - d + tt, :]

    gate_r = _sigmoid(_dot(xc, wr_ref[...]) + br_ref[...])
    gate_i = _sigmoid(_dot(xc, wi_ref[...]) + bi_ref[...])
    log_a = -LRU_C * gate_r * _softplus(-lam_ref[...])
    a = jnp.exp(log_a)
    u = jnp.sqrt(-jnp.tanh(log_a) * (a * a + 1.0)) * (gate_i * xc)

    row = lax.broadcasted_iota(jnp.int32, (tt, 1), 0)
    d = 1
    while d < tt:
        keep = row >= d
        a_sh = jnp.where(keep, pltpu.roll(a, d, axis=0), 1.0)
        u_sh = jnp.where(keep, pltpu.roll(u, d, axis=0), 0.0)
        u = a * u_sh + u
        a = a * a_sh
        d *= 2
    h = a * h_carry[0:1, :] + u
    h_carry[0:1, :] = h[tt - 1:tt, :]
    y_ref[0] = h * _gelu_tanh(gb_ref[0])


def _rglru(xb, gb, conv_w, conv_b, wr, br, wi, bi, lam, *, tt=256):
    B, S, W = xb.shape
    row = lambda a: a.reshape(1, -1)
    full = lambda a: pl.BlockSpec(a.shape, lambda b, s: (0,) * a.ndim)
    tok = pl.BlockSpec((1, tt, W), lambda b, s: (b, s, 0))
    small = [conv_w, row(conv_b), wr, row(br), wi, row(bi), row(lam)]
    return pl.pallas_call(
        _rglru_kernel,
        grid=(B, S // tt),
        in_specs=[tok, tok] + [full(a) for a in small],
        out_specs=tok,
        out_shape=jax.ShapeDtypeStruct((B, S, W), F32),
        scratch_shapes=[pltpu.VMEM((8, W), F32), pltpu.VMEM((8, W), F32)],
        compiler_params=_cparams(2),
        name="rglru",
    )(xb, gb, *small)


def _merge_kernel(x_ref, ya_ref, bonus_ref, g_ref, yb_ref, s1_ref, s2_ref, gt1_ref, sh2_ref,
                  sc2_ref, lng_ref, lnb_ref, bd_ref, pa_ref, pb_ref, wout_ref, gffn_ref,
                  wr_ref, br_ref, tri_ref, x1_out, h2_out, cw_out, meta_out, metat_out, cnt_out,
                  cnt_ref, *, sub_tiles):
    ya = ya_ref[0]
    bd = bd_ref[...]
    inv_n = 1.0 / HEAD
    mean = _dot_hl(ya, bd) * inv_n
    yc = ya - mean
    var = _dot_hl(yc * yc, bd) * inv_n
    yn = yc * lax.rsqrt(var + GN_EPS) * lng_ref[...] + lnb_ref[...]
    ya2 = (yn + bonus_ref[0]) * g_ref[0]

    merged = s1_ref[0] * _dot(ya2, pa_ref[...]) + s2_ref[0] * _dot(yb_ref[0], pb_ref[...])
    x1 = x_ref[0] + gt1_ref[0] * _dot(merged, wout_ref[...])
    x1_out[0] = x1

    ms = jnp.mean(x1 * x1, axis=-1, keepdims=True)
    h2 = (x1 * lax.rsqrt(ms + RMS_EPS)) * gffn_ref[...]
    h2 = h2 * (1.0 + sc2_ref[0]) + sh2_ref[0]
    h2_out[0] = h2.astype(BF16)

    logits = _dot3(h2, wr_ref[...]) + br_ref[...]
    lane = lax.broadcasted_iota(jnp.int32, logits.shape, 1).astype(F32)
    neg = jnp.float32(-jnp.inf)
    big = jnp.float32(1e9)
    is_g = lane < N_GROUPS
    lg = jnp.where(is_g, logits, neg)
    mg = jnp.max(lg, axis=-1, keepdims=True)
    pg_top = 1.0 / jnp.sum(jnp.exp(lg - mg), axis=-1, keepdims=True)
    g_idx = jnp.min(jnp.where(lg == mg, lane, big), axis=-1, keepdims=True)
    e_lane = lane - N_GROUPS
    in_grp = (e_lane >= g_idx * EXPERTS_PER_GROUP) & (e_lane < (g_idx + 1) * EXPERTS_PER_GROUP)
    le = jnp.where(in_grp, logits, neg)
    me = jnp.max(le, axis=-1, keepdims=True)
    ee = jnp.exp(le - me)
    pe = ee / jnp.sum(ee, axis=-1, keepdims=True)
    p1 = jnp.max(pe, axis=-1, keepdims=True)
    i1 = jnp.min(jnp.where(in_grp & (pe == p1), lane, big), axis=-1, keepdims=True)
    rest = jnp.where(in_grp & (lane != i1), pe, -1.0)
    p2 = jnp.max(rest, axis=-1, keepdims=True)
    i2 = jnp.min(jnp.where(rest == p2, lane, big), axis=-1, keepdims=True)
    den = p1 + p2
    cw = jnp.where(lane == i1, pg_top * (p1 / den), 0.0) + \
        jnp.where(lane == i2, pg_top * (p2 / den), 0.0)
    cw_out[0] = cw

    @pl.when(pl.program_id(1) % sub_tiles == 0)
    def _():
        cnt_ref[...] = jnp.zeros_like(cnt_ref)

    onehot = jnp.where(lane == g_idx, 1.0, 0.0)
    before = _dot(tri_ref[...], onehot) + cnt_ref[0:1, :]
    rank = jnp.sum(onehot * before, axis=-1, keepdims=True)
    cnt_ref[0:1, :] = cnt_ref[0:1, :] + jnp.sum(onehot, axis=0, keepdims=True)
    cnt_out[0, 0] = jnp.broadcast_to(cnt_ref[0:1, :], cnt_out.shape[2:])
    meta = jnp.where(lane == 0, g_idx, jnp.where(lane == 1, rank, 0.0))
    meta_out[0] = meta
    metat_out[0] = jnp.transpose(meta)[0:8, :]


def _merge(x, ya, bonus, g, yb, s1, s2, gt1, sh2, sc2, lnx_g, lnx_b, bd, pa, pb, wout, g_ffn,
           wr, br, *, tm=256, moe_tile=1024):
    B, S, D = x.shape
    A = ya.shape[-1]
    sub = moe_tile // tm
    row = lambda a: a.reshape(1, -1)
    full = lambda a: pl.BlockSpec(a.shape, lambda b, s: (0,) * a.ndim)
    tok = lambda n: pl.BlockSpec((1, tm, n), lambda b, s: (b, s, 0))
    per_b = pl.BlockSpec((1, 1, D), lambda b, s: (b, 0, 0))
    tri = (jnp.arange(tm)[:, None] > jnp.arange(tm)[None, :]).astype(BF16)
    small = [row(lnx_g), row(lnx_b), bd, pa, pb, wout, row(g_ffn), wr, row(br), tri]
    return pl.pallas_call(
        functools.partial(_merge_kernel, sub_tiles=sub),
        grid=(B, S // tm),
        in_specs=[tok(D), tok(A), tok(A), tok(A), tok(A), tok(D), tok(D), per_b, per_b, per_b]
        + [full(a) for a in small],
        out_specs=[tok(D), tok(D), tok(128), tok(128),
                   pl.BlockSpec((1, 8, tm), lambda b, s: (b, 0, s)),
                   pl.BlockSpec((1, 1, 8, 128), lambda b, s: (b, s // sub, 0, 0))],
        out_shape=[jax.ShapeDtypeStruct((B, S, D), F32),
                   jax.ShapeDtypeStruct((B, S, D), BF16),
                   jax.ShapeDtypeStruct((B, S, 128), F32),
                   jax.ShapeDtypeStruct((B, S, 128), F32),
                   jax.ShapeDtypeStruct((B, 8, S), F32),
                   jax.ShapeDtypeStruct((B, S // moe_tile, 8, 128), F32)],
        scratch_shapes=[pltpu.VMEM((8, 128), F32)],
        compiler_params=_cparams(2),
        name="merge_router",
    )(x, ya, bonus, g, yb, s1, s2, gt1.reshape(B, 1, D), sh2.reshape(B, 1, D),
      sc2.reshape(B, 1, D), *small)


def _moe_kernel(cnt_smem, x1_ref, h2_ref, cw_ref, meta_ref, metat_ref, w1_ref, w3_ref, w2_ref,
                gt2_ref, gf_ref, shf_ref, scf_ref, o_ref, acc_ref):
    b = pl.program_id(0)
    i = pl.program_id(1)
    g = pl.program_id(2)
    n_g = pl.num_programs(2)
    R = MOE_ROWS

    @pl.when(g == 0)
    def _():
        acc_ref[...] = jnp.zeros_like(acc_ref)

    gf = g.astype(F32)
    n_tok = cnt_smem[(b * pl.num_programs(1) + i) * n_g + g]
    grp_row = metat_ref[0, 0:1, :]
    rank_row = metat_ref[0, 1:2, :]
    meta = meta_ref[0]
    grp_col = meta[:, 0:1]
    rank_col = meta[:, 1:2]
    sel_row = jnp.where(grp_row == gf, rank_row, -1.0)
    sel_col = jnp.where(grp_col == gf, rank_col, -1.0)
    cw = cw_ref[0]
    cw_hi, cw_lo = _split(cw)
    lane = lax.broadcasted_iota(jnp.int32, (R, cw.shape[1]), 1)
    row_ids = lax.broadcasted_iota(jnp.int32, (R, 1), 0).astype(F32)
    lane_ids = lax.broadcasted_iota(jnp.int32, (1, R), 1).astype(F32)

    def block(rb, carry):
        r0 = (rb * R).astype(F32)
        p = jnp.where(sel_row == row_ids + r0, 1.0, 0.0).astype(BF16)
        xg = jnp.dot(p, h2_ref[0], preferred_element_type=F32).astype(BF16)
        cwg = jnp.dot(p, cw_hi, preferred_element_type=F32) + \
            jnp.dot(p, cw_lo, preferred_element_type=F32)
        y = jnp.zeros((R, acc_ref.shape[1]), F32)
        for e in range(EXPERTS_PER_GROUP):
            he = _silu(jnp.dot(xg, w1_ref[e], preferred_element_type=F32)) * \
                jnp.dot(xg, w3_ref[e], preferred_element_type=F32)
            oe = _dot(he, w2_ref[e])
            col = jnp.sum(jnp.where(lane == N_GROUPS + g * EXPERTS_PER_GROUP + e, cwg, 0.0),
                          axis=-1, keepdims=True)
            y = y + col * oe
        pt = jnp.where(sel_col == lane_ids + r0, 1.0, 0.0).astype(BF16)
        acc_ref[...] += jnp.dot(pt, y.astype(BF16), preferred_element_type=F32)
        return carry

    lax.fori_loop(0, (n_tok + R - 1) // R, block, 0)

    @pl.when(g == n_g - 1)
    def _():
        x2 = x1_ref[0] + gt2_ref[0] * acc_ref[...]
        ms = jnp.mean(x2 * x2, axis=-1, keepdims=True)
        y = (x2 * lax.rsqrt(ms + RMS_EPS)) * gf_ref[...]
        o_ref[0] = y * (1.0 + scf_ref[0]) + shf_ref[0]


def _moe(counts, x1, h2, cw, meta, metat, w1, w3, w2, gt2, g_final, shf, scf, *, tm):
    B, S, D = x1.shape
    E, _, F = w1.shape
    G = EXPERTS_PER_GROUP
    tok = lambda n: pl.BlockSpec((1, tm, n), lambda b, s, g, c: (b, s, 0))
    per_b = pl.BlockSpec((1, 1, D), lambda b, s, g, c: (b, 0, 0))
    grid_spec = pltpu.PrefetchScalarGridSpec(
        num_scalar_prefetch=1,
        grid=(B, S // tm, E // G),
        in_specs=[tok(D), tok(D), tok(128), tok(128),
                  pl.BlockSpec((1, 8, tm), lambda b, s, g, c: (b, 0, s)),
                  pl.BlockSpec((G, D, F), lambda b, s, g, c: (g, 0, 0)),
                  pl.BlockSpec((G, D, F), lambda b, s, g, c: (g, 0, 0)),
                  pl.BlockSpec((G, F, D), lambda b, s, g, c: (g, 0, 0)),
                  per_b, pl.BlockSpec((1, D), lambda b, s, g, c: (0, 0)), per_b, per_b],
        out_specs=tok(D),
        scratch_shapes=[pltpu.VMEM((tm, D), F32)])
    return pl.pallas_call(
        _moe_kernel,
        grid_spec=grid_spec,
        out_shape=jax.ShapeDtypeStruct((B, S, D), F32),
        compiler_params=_cparams(3),
        name="moe",
    )(counts, x1, h2, cw, meta, metat, w1, w3, w2, gt2.reshape(B, 1, D), g_final.reshape(1, D),
      shf.reshape(B, 1, D), scf.reshape(B, 1, D))


def _block_diag(w):
    G, n, _ = w.shape
    eye = jnp.eye(G, dtype=w.dtype)
    return (eye[:, None, :, None] * w[:, :, None, :]).reshape(G * n, G * n)


def kernel(x, c, w_ada, b_ada, g_mix, w_in, mu_rkv, mu_wag, w0, w1, w2, a0, a1, a2, g1, g2, k_k, k_a, r_k, lnx_g, lnx_b, conv_w, conv_b, w_rgate, b_rgate, w_igate, b_igate, lam, p_a, p_b, w_out, g_ffn, w_rg, b_rg, w_re, b_re, w1e, w3e, w2e, g_final, w_ada_f, b_ada_f):
    B, S, D = x.shape
    depth = w_ada.shape[0]
    assert depth == 1, "the final norm is fused into the last MoE call; one layer supported"
    A = w0.shape[-1]
    Bw = lam.shape[-1]

    c8 = jnp.zeros((8, D), F32).at[:B].set(c)
    modf = _ada(c8, w_ada_f, b_ada_f)[:B]
    shf, scf = modf[:, :D], modf[:, D:]

    head_ids = jnp.arange(A) // HEAD
    bd = (head_ids[:, None] == head_ids[None, :]).astype(BF16)

    for l in range(depth):
        mod = _ada(c8, w_ada[l], b_ada[l])[:B]
        sh1, sc1, gt1, sh2, sc2, gt2 = (mod[:, i * D:(i + 1) * D] for i in range(6))

        lowrank = jnp.concatenate([w1[l], a1[l], g1[l]], axis=1)
        mu_cols = jnp.concatenate([jnp.broadcast_to(mu_wag[l, 0][:, None], w1[l].shape),
                                   jnp.broadcast_to(mu_wag[l, 1][:, None], a1[l].shape),
                                   jnp.broadcast_to(mu_wag[l, 2][:, None], g1[l].shape)], axis=1)
        o1 = 3 * A
        wext = jnp.concatenate([w_in[l][:, :o1], mu_cols * lowrank, (1.0 - mu_cols) * lowrank,
                                w_in[l][:, o1:]], axis=1).astype(BF16)
        ra, rb = w2.shape[1], g2.shape[1]
        w2cat = jnp.zeros((2 * ra + rb, 3 * A), F32)
        w2cat = w2cat.at[0:ra, 0:A].set(w2[l]).at[ra:2 * ra, A:2 * A].set(a2[l])
        w2cat = w2cat.at[2 * ra:, 2 * A:].set(g2[l]).astype(BF16)

        (r, lw, k, v, kk, bvec, g, bonus, xb, gb, s1, s2) = _inproj(
            x, sh1, sc1, g_mix[l], wext, mu_rkv[l], w0[l], a0[l], w2cat, k_k[l], k_a[l],
            r_k[l].reshape(-1), bd)

        ya = _rwkv(r, lw, k, v, kk, bvec)
        yb = _rglru(xb, gb, conv_w[l].reshape(CONV_WIDTH, Bw), conv_b[l],
                    _block_diag(w_rgate[l]).astype(BF16), b_rgate[l],
                    _block_diag(w_igate[l]).astype(BF16), b_igate[l], lam[l])

        n_g, n_e = w_rg.shape[-1], w_re.shape[-1]
        wr = jnp.zeros((D, 128), F32).at[:, :n_g].set(w_rg[l]).at[:, n_g:n_g + n_e].set(w_re[l])
        br = jnp.zeros((128,), F32).at[:n_g].set(b_rg[l]).at[n_g:n_g + n_e].set(b_re[l])
        x1, h2, cw, meta, metat, cnt = _merge(
            x, ya, bonus, g, yb, s1, s2, gt1, sh2, sc2, lnx_g[l], lnx_b[l], bd,
            p_a[l].astype(BF16), p_b[l].astype(BF16), w_out[l].astype(BF16),
            g_ffn[l], wr, br, moe_tile=MOE_TILE)
        counts = cnt[:, :, 0, :N_GROUPS].astype(jnp.int32).reshape(-1)
        x = _moe(counts, x1, h2, cw, meta, metat, w1e[l].astype(BF16), w3e[l].astype(BF16),
                 w2e[l].astype(BF16), gt2, g_final, shf, scf, tm=MOE_TILE)
    return x
```

```python
import functools

import jax
import jax.numpy as jnp
from jax import lax
from jax.experimental import pallas as pl
from jax.experimental.pallas import tpu as pltpu

F32 = jnp.float32
BF16 = jnp.bfloat16

HEAD = 64
CHUNK = 64
SUB = 16
QUAD = 4
RMS_EPS = 1e-6
GN_EPS = 64e-5
LRU_C = 8.0
CONV_WIDTH = 4
N_GROUPS = 4
EXPERTS_PER_GROUP = 8
MOE_TILE = 1024
MOE_ROWS = 128
VMEM_LIMIT = 56 * 1024 * 1024


def _cparams(n_axes):
    return pltpu.CompilerParams(dimension_semantics=("arbitrary",) * n_axes,
                                vmem_limit_bytes=VMEM_LIMIT)


def _dot(a, b, dims=(((1,), (0,)), ((), ()))):
    return lax.dot_general(a.astype(BF16), b.astype(BF16), dims,
                           preferred_element_type=F32)


NT = (((1,), (1,)), ((), ()))
TN = (((0,), (0,)), ((), ()))


def _split(a):
    hi = a.astype(BF16)
    lo = (a - hi.astype(F32)).astype(BF16)
    return hi, lo


def _dot_hl(a, b_exact, dims=(((1,), (0,)), ((), ()))):
    hi, lo = _split(a)
    return (lax.dot_general(hi, b_exact, dims, preferred_element_type=F32)
            + lax.dot_general(lo, b_exact, dims, preferred_element_type=F32))


def _head_sums(a, ones_blocks):
    w = ones_blocks.shape[0]
    parts = [_dot_hl(a[:, o:o + w], ones_blocks) for o in range(0, a.shape[1], w)]
    return parts[0] if len(parts) == 1 else jnp.concatenate(parts, axis=1)


def _dot3(a, b, dims=(((1,), (0,)), ((), ()))):
    ah, al = _split(a)
    bh, bl = _split(b)
    d = functools.partial(lax.dot_general, dimension_numbers=dims,
                          preferred_element_type=F32)
    return d(ah, bh) + (d(ah, bl) + d(al, bh))


def _sigmoid(z):
    return 1.0 / (1.0 + jnp.exp(-z))


def _softplus(z):
    return jnp.maximum(z, 0.0) + jnp.log1p(jnp.exp(-jnp.abs(z)))


def _silu(z):
    return z * _sigmoid(z)


def _gelu_tanh(z):
    return 0.5 * z * (1.0 + jnp.tanh(0.7978845608028654 * (z + 0.044715 * (z * z * z))))


def _ada_kernel(c_ref, w_ref, b_ref, o_ref):
    ca = _silu(c_ref[...])
    o_ref[...] = _dot3(ca, w_ref[...]) + b_ref[...]


def _ada(c8, w, b, tn=1024):
    d, n = w.shape
    return pl.pallas_call(
        _ada_kernel,
        grid=(n // tn,),
        in_specs=[pl.BlockSpec((8, d), lambda j: (0, 0)),
                  pl.BlockSpec((d, tn), lambda j: (0, j)),
                  pl.BlockSpec((1, tn), lambda j: (0, j))],
        out_specs=pl.BlockSpec((8, tn), lambda j: (0, j)),
        out_shape=jax.ShapeDtypeStruct((8, n), F32),
        compiler_params=_cparams(1),
        name="adaln",
    )(c8, w, b.reshape(1, n))


def _inproj_kernel(x_ref, sh_ref, sc_ref, g_ref, wext_ref, mu_ref, w0_ref, a0_ref,
                   w2cat_ref, kk_ref, ka_ref, rk_ref, bd_ref,
                   r_out, lw_out, k_out, v_out, kk_out, b_out, g_out, bonus_out,
                   xb_out, gb_out, s1_out, s2_out, carry_ref, *, a_width, d_model):
    A = a_width
    n_shift = 3 * A + 256
    s = pl.program_id(1)

    @pl.when(s == 0)
    def _():
        carry_ref[...] = jnp.zeros_like(carry_ref)

    x = x_ref[0]
    tm = x.shape[0]
    ms = jnp.mean(x * x, axis=-1, keepdims=True)
    h = (x * lax.rsqrt(ms + RMS_EPS)) * g_ref[...]
    h = h * (1.0 + sc_ref[0]) + sh_ref[0]
    hb = h.astype(BF16)

    sg = jnp.dot(hb, wext_ref[:, 0:n_shift + 256], preferred_element_type=F32)
    cur = sg[:, 0:n_shift]
    row = lax.broadcasted_iota(jnp.int32, (tm, 1), 0)
    prev = jnp.where(row == 0, carry_ref[0:1, :], pltpu.roll(cur, 1, axis=0))
    carry_ref[0:1, :] = cur[tm - 1:tm, :]

    rkv = cur[:, 0:3 * A]
    rkv = rkv + (prev[:, 0:3 * A] - rkv) * mu_ref[...]
    r = rkv[:, 0:A]
    k = rkv[:, A:2 * A]
    v = rkv[:, 2 * A:3 * A]

    pre = sg[:, n_shift:n_shift + 256] + prev[:, 3 * A:3 * A + 256]
    lane = lax.broadcasted_iota(jnp.int32, pre.shape, 1)
    act = jnp.where(lane < 64, jnp.tanh(pre), jnp.where(lane < 128, pre, _sigmoid(pre)))
    low = _dot(act, w2cat_ref[...])
    w_log = -_softplus(-(w0_ref[...] + low[:, 0:A])) - 0.5
    lw_out[0] = -jnp.exp(w_log)
    iclr = _sigmoid(a0_ref[...] + low[:, A:2 * A])
    g_out[0] = low[:, 2 * A:3 * A]

    bd = bd_ref[...]
    kkr = k * kk_ref[...]
    kkn = kkr * lax.rsqrt(_head_sums(kkr * kkr, bd) + 1e-12)
    k2 = k * (1.0 + (iclr - 1.0) * ka_ref[...])
    r_out[0] = r
    k_out[0] = k2
    v_out[0] = v
    kk_out[0] = kkn
    b_out[0] = kkn * iclr
    bonus_out[0] = _head_sums(r * k2 * rk_ref[...], bd) * v

    o = n_shift + 256
    bw = xb_out.shape[-1]
    xg = jnp.dot(hb, wext_ref[:, o:o + 2 * bw], preferred_element_type=F32)
    xb_out[0] = xg[:, 0:bw]
    gb_out[0] = xg[:, bw:2 * bw]
    o += 2 * bw
    s1_out[0] = _sigmoid(jnp.dot(hb, wext_ref[:, o:o + d_model], preferred_element_type=F32))
    o += d_model
    s2_out[0] = _sigmoid(jnp.dot(hb, wext_ref[:, o:o + d_model], preferred_element_type=F32))


def _inproj(x, sh1, sc1, g_mix, wext, mu_rkv, w0, a0, w2cat, k_k, k_a, r_k, bd, *, tm=256):
    B, S, D = x.shape
    A = w0.shape[-1]
    Bw = A
    ncol = wext.shape[1]
    row = lambda a: a.reshape(1, -1)
    full = lambda a: pl.BlockSpec(a.shape, lambda b, s: (0,) * a.ndim)
    tok = lambda n: pl.BlockSpec((1, tm, n), lambda b, s: (b, s, 0))
    per_b = pl.BlockSpec((1, 1, D), lambda b, s: (b, 0, 0))
    small = [row(g_mix), wext, row(mu_rkv), row(w0), row(a0), w2cat, row(k_k), row(k_a),
             row(r_k), bd]
    outs = [jax.ShapeDtypeStruct((B, S, A), F32)] * 8 + \
           [jax.ShapeDtypeStruct((B, S, Bw), F32)] * 2 + \
           [jax.ShapeDtypeStruct((B, S, D), F32)] * 2
    return pl.pallas_call(
        functools.partial(_inproj_kernel, a_width=A, d_model=D),
        grid=(B, S // tm),
        in_specs=[tok(D), per_b, per_b] + [full(a) for a in small],
        out_specs=[tok(A)] * 8 + [tok(Bw)] * 2 + [tok(D)] * 2,
        out_shape=outs,
        scratch_shapes=[pltpu.VMEM((8, 3 * A + 256), F32)],
        compiler_params=_cparams(2),
        name="inproj",
    )(x, sh1.reshape(B, 1, D), sc1.reshape(B, 1, D), *small)


def _rwkv_kernel(r_ref, lw_ref, k_ref, v_ref, kk_ref, b_ref, tri_ref, bdm_ref, slm_ref,
                 lm_ref, dgm_ref, y_ref, s_ref):
    @pl.when(pl.program_id(0) == 0)
    def _():
        s_ref[...] = jnp.zeros_like(s_ref)

    W = QUAD * HEAD
    n_b, _, n_a = r_ref.shape
    masks = (tri_ref[...], bdm_ref[...], slm_ref[...], lm_ref[...], dgm_ref[...])
    chains = []
    for bi in range(n_b):
        for q in range(n_a // W):
            cols = slice(q * W, (q + 1) * W)
            ins = [ref[bi, :, cols] for ref in (r_ref, lw_ref, k_ref, v_ref, kk_ref, b_ref)]
            chains.append(_rwkv_chunk(*ins, s_ref, bi * (n_a // W) + q, masks,
                                      functools.partial(_store_y, y_ref, bi, cols)))
    while chains:
        chains = [c for c in chains if next(c, "done") != "done"]


def _store_y(y_ref, bi, cols, y):
    y_ref[bi, :, cols] = y


def _rwkv_chunk(r, lw, k, v, kk, b, s_ref, idx, masks, store_y):
    tri, bdm, slm, lm, dgm = masks
    L = CHUNK
    a = -kk

    cl = _dot_hl3(tri, lw)
    clp = cl - lw
    cm = cl[L // 2 - 1:L // 2, :]
    ce = cl[L - 1:L, :]
    at = a * jnp.exp(clp - cm)
    rt = r * jnp.exp(cl - cm)
    e_inv = jnp.exp(cm - cl)
    bt = b * e_inv
    kt = k * e_inv
    e_end = jnp.exp(ce - cl)
    bh = b * e_end
    kh = k * e_end
    g_mid = jnp.exp(cm)
    g_end = jnp.exp(ce)

    tile = lambda m: jnp.concatenate([m] * QUAD, axis=0)
    stack = lambda m: tile(m) * bdm
    a_s = stack(at)
    r_s = stack(rt)
    v_s = stack(v)
    bh_s = stack(bh)
    kh_s = stack(kh)
    bt_t = tile(bt)
    kt_t = tile(kt)
    yield

    a_ab = _dot(a_s, bt_t, NT) * slm
    a_ak = _dot(a_s, kt_t, NT) * slm
    a_rb = _dot(r_s, bt_t, NT) * lm
    a_rk = _dot(r_s, kt_t, NT) * lm
    yield

    d = a_ab * dgm
    e = a_ab - d
    p = d
    d_inv = (lm - slm) + p
    n = 2
    while n < SUB:
        p = _dot(p, p)
        yield
        d_inv = d_inv + _dot(d_inv, p)
        yield
        n *= 2
    f = _dot(d_inv, e)
    yield
    t_inv = d_inv
    fp = f
    n = 1
    terms = []
    while n < L // SUB:
        terms.append(fp)
        n *= 2
        if n < L // SUB:
            fp = _dot(fp, fp)
            yield
    for fp in reversed(terms):
        t_inv = t_inv + _dot(fp, t_inv)
        yield

    s0 = s_ref[idx]
    s_mid = s0 * g_mid
    rhs = _dot(a_s, s_mid, NT) + _dot(a_ak, v_s)
    y_part = _dot(r_s, s_mid, NT) + _dot(a_rk, v_s)
    yield
    u = _dot(t_inv, rhs)
    yield
    y_s = y_part + _dot(a_rb, u)
    s_ref[idx] = s0 * g_end + _dot(u, bh_s, TN) + _dot(v_s, kh_s, TN)
    y = y_s[0:L]
    for q in range(1, QUAD):
        y = y + y_s[q * L:(q + 1) * L]
    store_y(y)


def _dot_hl3(tri, lw):
    hi = lw.astype(BF16)
    r1 = lw - hi.astype(F32)
    mid = r1.astype(BF16)
    lo = (r1 - mid.astype(F32)).astype(BF16)
    d = functools.partial(jnp.dot, preferred_element_type=F32)
    return d(tri, hi) + (d(tri, mid) + d(tri, lo))


def _rwkv(r, lw, k, v, kk, b):
    B, S, A = r.shape
    W = QUAD * HEAD
    L = CHUNK
    idx = jnp.arange(W)
    assert L == HEAD, "stacked rows (head, time) and lanes (head, channel) share one block size"
    same = (idx[:, None] // L) == (idx[None, :] // L)
    ti = idx[:, None] % L
    tj = idx[None, :] % L
    bdm = same.astype(F32)
    slm = (same & (ti > tj)).astype(F32)
    lm = (same & (ti >= tj)).astype(F32)
    dgm = (same & (ti > tj) & (ti // SUB == tj // SUB)).astype(F32)
    tri = (jnp.arange(L)[:, None] >= jnp.arange(L)[None, :]).astype(BF16)
    tok = pl.BlockSpec((B, L, A), lambda c: (0, c, 0))
    full = lambda a: pl.BlockSpec(a.shape, lambda c: (0,) * a.ndim)
    return pl.pallas_call(
        _rwkv_kernel,
        grid=(S // L,),
        in_specs=[tok] * 6 + [full(tri), full(bdm), full(slm), full(lm), full(dgm)],
        out_specs=tok,
        out_shape=jax.ShapeDtypeStruct((B, S, A), F32),
        scratch_shapes=[pltpu.VMEM((B * (A // W), W, W), F32)],
        compiler_params=_cparams(1),
        name="rwkv7",
    )(r, lw, k, v, kk, b, tri, bdm, slm, lm, dgm)


def _rglru_kernel(xb_ref, gb_ref, cw_ref, cb_ref, wr_ref, br_ref, wi_ref, bi_ref, lam_ref,
                  y_ref, xc_carry, h_carry):
    s = pl.program_id(1)

    @pl.when(s == 0)
    def _():
        xc_carry[...] = jnp.zeros_like(xc_carry)
        h_carry[...] = jnp.zeros_like(h_carry)

    xb = xb_ref[0]
    tt = xb.shape[0]
    ext = jnp.concatenate([xc_carry[...], xb], axis=0)
    xc_carry[...] = xb[tt - 8:tt, :]
    xc = cb_ref[...] + cw_ref[CONV_WIDTH - 1:CONV_WIDTH, :] * xb
    for d in range(1, CONV_WIDTH):
        xc = xc + cw_ref[CONV_WIDTH - 1 - d:CONV_WIDTH - d, :] * ext[8 - d:8 - d + tt, :]

    gate_r = _sigmoid(_dot(xc, wr_ref[...]) + br_ref[...])
    gate_i = _sigmoid(_dot(xc, wi_ref[...]) + bi_ref[...])
    log_a = -LRU_C * gate_r * _softplus(-lam_ref[...])
    a = jnp.exp(log_a)
    u = jnp.sqrt(-jnp.tanh(log_a) * (a * a + 1.0)) * (gate_i * xc)

    row = lax.broadcasted_iota(jnp.int32, (tt, 1), 0)
    d = 1
    while d < tt:
        keep = row >= d
        a_sh = jnp.where(keep, pltpu.roll(a, d, axis=0), 1.0)
        u_sh = jnp.where(keep, pltpu.roll(u, d, axis=0), 0.0)
        u = a * u_sh + u
        a = a * a_sh
        d *= 2
    h = a * h_carry[0:1, :] + u
    h_carry[0:1, :] = h[tt - 1:tt, :]
    y_ref[0] = h * _gelu_tanh(gb_ref[0])


def _rglru(xb, gb, conv_w, conv_b, wr, br, wi, bi, lam, *, tt=256):
    B, S, W = xb.shape
    row = lambda a: a.reshape(1, -1)
    full = lambda a: pl.BlockSpec(a.shape, lambda b, s: (0,) * a.ndim)
    tok = pl.BlockSpec((1, tt, W), lambda b, s: (b, s, 0))
    small = [conv_w, row(conv_b), wr, row(br), wi, row(bi), row(lam)]
    return pl.pallas_call(
        _rglru_kernel,
        grid=(B, S // tt),
        in_specs=[tok, tok] + [full(a) for a in small],
        out_specs=tok,
        out_shape=jax.ShapeDtypeStruct((B, S, W), F32),
        scratch_shapes=[pltpu.VMEM((8, W), F32), pltpu.VMEM((8, W), F32)],
        compiler_params=_cparams(2),
        name="rglru",
    )(xb, gb, *small)


def _merge_kernel(x_ref, ya_ref, bonus_ref, g_ref, yb_ref, s1_ref, s2_ref, gt1_ref, sh2_ref,
                  sc2_ref, lng_ref, lnb_ref, bd_ref, pa_ref, pb_ref, wout_ref, gffn_ref,
                  wr_ref, br_ref, tri_ref, x1_out, h2_out, cw_out, meta_out, metat_out, cnt_out,
                  cnt_ref, *, sub_tiles):
    ya = ya_ref[0]
    bd = bd_ref[...]
    inv_n = 1.0 / HEAD
    mean = _head_sums(ya, bd) * inv_n
    yc = ya - mean
    var = _head_sums(yc * yc, bd) * inv_n
    yn = yc * lax.rsqrt(var + GN_EPS) * lng_ref[...] + lnb_ref[...]
    ya2 = (yn + bonus_ref[0]) * g_ref[0]

    merged = s1_ref[0] * _dot(ya2, pa_ref[...]) + s2_ref[0] * _dot(yb_ref[0], pb_ref[...])
    x1 = x_ref[0] + gt1_ref[0] * _dot(merged, wout_ref[...])
    x1_out[0] = x1

    ms = jnp.mean(x1 * x1, axis=-1, keepdims=True)
    h2 = (x1 * lax.rsqrt(ms + RMS_EPS)) * gffn_ref[...]
    h2 = h2 * (1.0 + sc2_ref[0]) + sh2_ref[0]
    h2_out[0] = h2.astype(BF16)

    logits = _dot3(h2, wr_ref[...]) + br_ref[...]
    lane = lax.broadcasted_iota(jnp.int32, logits.shape, 1).astype(F32)
    neg = jnp.float32(-jnp.inf)
    big = jnp.float32(1e9)
    is_g = lane < N_GROUPS
    lg = jnp.where(is_g, logits, neg)
    mg = jnp.max(lg, axis=-1, keepdims=True)
    pg_top = 1.0 / jnp.sum(jnp.exp(lg - mg), axis=-1, keepdims=True)
    g_idx = jnp.min(jnp.where(lg == mg, lane, big), axis=-1, keepdims=True)
    e_lane = lane - N_GROUPS
    in_grp = (e_lane >= g_idx * EXPERTS_PER_GROUP) & (e_lane < (g_idx + 1) * EXPERTS_PER_GROUP)
    le = jnp.where(in_grp, logits, neg)
    me = jnp.max(le, axis=-1, keepdims=True)
    ee = jnp.exp(le - me)
    pe = ee / jnp.sum(ee, axis=-1, keepdims=True)
    p1 = jnp.max(pe, axis=-1, keepdims=True)
    i1 = jnp.min(jnp.where(in_grp & (pe == p1), lane, big), axis=-1, keepdims=True)
    rest = jnp.where(in_grp & (lane != i1), pe, -1.0)
    p2 = jnp.max(rest, axis=-1, keepdims=True)
    i2 = jnp.min(jnp.where(rest == p2, lane, big), axis=-1, keepdims=True)
    den = p1 + p2
    cw = jnp.where(lane == i1, pg_top * (p1 / den), 0.0) + \
        jnp.where(lane == i2, pg_top * (p2 / den), 0.0)
    cw_out[0] = cw

    @pl.when(pl.program_id(1) % sub_tiles == 0)
    def _():
        cnt_ref[...] = jnp.zeros_like(cnt_ref)

    onehot = jnp.where(lane == g_idx, 1.0, 0.0)
    before = _dot(tri_ref[...], onehot) + cnt_ref[0:1, :]
    rank = jnp.sum(onehot * before, axis=-1, keepdims=True)
    cnt_ref[0:1, :] = cnt_ref[0:1, :] + jnp.sum(onehot, axis=0, keepdims=True)
    cnt_out[0, 0] = jnp.broadcast_to(cnt_ref[0:1, :], cnt_out.shape[2:])
    meta = jnp.where(lane == 0, g_idx, jnp.where(lane == 1, rank, 0.0))
    meta_out[0] = meta
    metat_out[0] = jnp.transpose(meta)[0:8, :]


def _merge(x, ya, bonus, g, yb, s1, s2, gt1, sh2, sc2, lnx_g, lnx_b, bd, pa, pb, wout, g_ffn,
           wr, br, *, tm=256, moe_tile=1024):
    B, S, D = x.shape
    A = ya.shape[-1]
    sub = moe_tile // tm
    row = lambda a: a.reshape(1, -1)
    full = lambda a: pl.BlockSpec(a.shape, lambda b, s: (0,) * a.ndim)
    tok = lambda n: pl.BlockSpec((1, tm, n), lambda b, s: (b, s, 0))
    per_b = pl.BlockSpec((1, 1, D), lambda b, s: (b, 0, 0))
    tri = (jnp.arange(tm)[:, None] > jnp.arange(tm)[None, :]).astype(BF16)
    small = [row(lnx_g), row(lnx_b), bd, pa, pb, wout, row(g_ffn), wr, row(br), tri]
    return pl.pallas_call(
        functools.partial(_merge_kernel, sub_tiles=sub),
        grid=(B, S // tm),
        in_specs=[tok(D), tok(A), tok(A), tok(A), tok(A), tok(D), tok(D), per_b, per_b, per_b]
        + [full(a) for a in small],
        out_specs=[tok(D), tok(D), tok(128), tok(128),
                   pl.BlockSpec((1, 8, tm), lambda b, s: (b, 0, s)),
                   pl.BlockSpec((1, 1, 8, 128), lambda b, s: (b, s // sub, 0, 0))],
        out_shape=[jax.ShapeDtypeStruct((B, S, D), F32),
                   jax.ShapeDtypeStruct((B, S, D), BF16),
                   jax.ShapeDtypeStruct((B, S, 128), F32),
                   jax.ShapeDtypeStruct((B, S, 128), F32),
                   jax.ShapeDtypeStruct((B, 8, S), F32),
                   jax.ShapeDtypeStruct((B, S // moe_tile, 8, 128), F32)],
        scratch_shapes=[pltpu.VMEM((8, 128), F32)],
        compiler_params=_cparams(2),
        name="merge_router",
    )(x, ya, bonus, g, yb, s1, s2, gt1.reshape(B, 1, D), sh2.reshape(B, 1, D),
      sc2.reshape(B, 1, D), *small)


def _moe_kernel(cnt_smem, x1_ref, h2_ref, cw_ref, meta_ref, metat_ref, w1_ref, w3_ref, w2_ref,
                gt2_ref, gf_ref, shf_ref, scf_ref, o_ref, acc_ref):
    b = pl.program_id(0)
    i = pl.program_id(1)
    g = pl.program_id(2)
    n_g = pl.num_programs(2)
    R = MOE_ROWS

    @pl.when(g == 0)
    def _():
        acc_ref[...] = jnp.zeros_like(acc_ref)

    gf = g.astype(F32)
    n_tok = cnt_smem[(b * pl.num_programs(1) + i) * n_g + g]
    grp_row = metat_ref[0, 0:1, :]
    rank_row = metat_ref[0, 1:2, :]
    meta = meta_ref[0]
    grp_col = meta[:, 0:1]
    rank_col = meta[:, 1:2]
    sel_row = jnp.where(grp_row == gf, rank_row, -1.0)
    sel_col = jnp.where(grp_col == gf, rank_col, -1.0)
    cw = cw_ref[0]
    cw_hi, cw_lo = _split(cw)
    lane = lax.broadcasted_iota(jnp.int32, (R, cw.shape[1]), 1)
    row_ids = lax.broadcasted_iota(jnp.int32, (R, 1), 0).astype(F32)
    lane_ids = lax.broadcasted_iota(jnp.int32, (1, R), 1).astype(F32)

    def block(rb, carry):
        r0 = (rb * R).astype(F32)
        p = jnp.where(sel_row == row_ids + r0, 1.0, 0.0).astype(BF16)
        xg = jnp.dot(p, h2_ref[0], preferred_element_type=F32).astype(BF16)
        cwg = jnp.dot(p, cw_hi, preferred_element_type=F32) + \
            jnp.dot(p, cw_lo, preferred_element_type=F32)
        hs = []
        for e in range(EXPERTS_PER_GROUP):
            he = _silu(jnp.dot(xg, w1_ref[e], preferred_element_type=F32)) * \
                jnp.dot(xg, w3_ref[e], preferred_element_type=F32)
            col = jnp.sum(jnp.where(lane == N_GROUPS + g * EXPERTS_PER_GROUP + e, cwg, 0.0),
                          axis=-1, keepdims=True)
            hs.append((col * he).astype(BF16))
        y = jnp.dot(jnp.concatenate(hs, axis=1), w2_ref[...].reshape(-1, w2_ref.shape[-1]),
                    preferred_element_type=F32)
        pt = jnp.where(sel_col == lane_ids + r0, 1.0, 0.0).astype(BF16)
        acc_ref[...] += jnp.dot(pt, y.astype(BF16), preferred_element_type=F32)
        return carry

    lax.fori_loop(0, (n_tok + R - 1) // R, block, 0)

    @pl.when(g == n_g - 1)
    def _():
        x2 = x1_ref[0] + gt2_ref[0] * acc_ref[...]
        ms = jnp.mean(x2 * x2, axis=-1, keepdims=True)
        y = (x2 * lax.rsqrt(ms + RMS_EPS)) * gf_ref[...]
        o_ref[0] = y * (1.0 + scf_ref[0]) + shf_ref[0]


def _moe(counts, x1, h2, cw, meta, metat, w1, w3, w2, gt2, g_final, shf, scf, *, tm):
    B, S, D = x1.shape
    E, _, F = w1.shape
    G = EXPERTS_PER_GROUP
    tok = lambda n: pl.BlockSpec((1, tm, n), lambda b, s, g, c: (b, s, 0))
    per_b = pl.BlockSpec((1, 1, D), lambda b, s, g, c: (b, 0, 0))
    grid_spec = pltpu.PrefetchScalarGridSpec(
        num_scalar_prefetch=1,
        grid=(B, S // tm, E // G),
        in_specs=[tok(D), tok(D), tok(128), tok(128),
                  pl.BlockSpec((1, 8, tm), lambda b, s, g, c: (b, 0, s)),
                  pl.BlockSpec((G, D, F), lambda b, s, g, c: (g, 0, 0)),
                  pl.BlockSpec((G, D, F), lambda b, s, g, c: (g, 0, 0)),
                  pl.BlockSpec((G, F, D), lambda b, s, g, c: (g, 0, 0)),
                  per_b, pl.BlockSpec((1, D), lambda b, s, g, c: (0, 0)), per_b, per_b],
        out_specs=tok(D),
        scratch_shapes=[pltpu.VMEM((tm, D), F32)])
    return pl.pallas_call(
        _moe_kernel,
        grid_spec=grid_spec,
        out_shape=jax.ShapeDtypeStruct((B, S, D), F32),
        compiler_params=_cparams(3),
        name="moe",
    )(counts, x1, h2, cw, meta, metat, w1, w3, w2, gt2.reshape(B, 1, D), g_final.reshape(1, D),
      shf.reshape(B, 1, D), scf.reshape(B, 1, D))


def _block_diag(w):
    G, n, _ = w.shape
    eye = jnp.eye(G, dtype=w.dtype)
    return (eye[:, None, :, None] * w[:, :, None, :]).reshape(G * n, G * n)


def kernel(x, c, w_ada, b_ada, g_mix, w_in, mu_rkv, mu_wag, w0, w1, w2, a0, a1, a2, g1, g2, k_k, k_a, r_k, lnx_g, lnx_b, conv_w, conv_b, w_rgate, b_rgate, w_igate, b_igate, lam, p_a, p_b, w_out, g_ffn, w_rg, b_rg, w_re, b_re, w1e, w3e, w2e, g_final, w_ada_f, b_ada_f):
    B, S, D = x.shape
    depth = w_ada.shape[0]
    assert depth == 1, "the final norm is fused into the last MoE call; one layer supported"
    A = w0.shape[-1]
    Bw = lam.shape[-1]

    c8 = jnp.zeros((8, D), F32).at[:B].set(c)
    modf = _ada(c8, w_ada_f, b_ada_f)[:B]
    shf, scf = modf[:, :D], modf[:, D:]

    head_ids = jnp.arange(QUAD * HEAD) // HEAD
    bd = (head_ids[:, None] == head_ids[None, :]).astype(BF16)

    for l in range(depth):
        mod = _ada(c8, w_ada[l], b_ada[l])[:B]
        sh1, sc1, gt1, sh2, sc2, gt2 = (mod[:, i * D:(i + 1) * D] for i in range(6))

        lowrank = jnp.concatenate([w1[l], a1[l], g1[l]], axis=1)
        mu_cols = jnp.concatenate([jnp.broadcast_to(mu_wag[l, 0][:, None], w1[l].shape),
                                   jnp.broadcast_to(mu_wag[l, 1][:, None], a1[l].shape),
                                   jnp.broadcast_to(mu_wag[l, 2][:, None], g1[l].shape)], axis=1)
        o1 = 3 * A
        wext = jnp.concatenate([w_in[l][:, :o1], mu_cols * lowrank, (1.0 - mu_cols) * lowrank,
                                w_in[l][:, o1:]], axis=1).astype(BF16)
        ra, rb = w2.shape[1], g2.shape[1]
        w2cat = jnp.zeros((2 * ra + rb, 3 * A), F32)
        w2cat = w2cat.at[0:ra, 0:A].set(w2[l]).at[ra:2 * ra, A:2 * A].set(a2[l])
        w2cat = w2cat.at[2 * ra:, 2 * A:].set(g2[l]).astype(BF16)

        (r, lw, k, v, kk, bvec, g, bonus, xb, gb, s1, s2) = _inproj(
            x, sh1, sc1, g_mix[l], wext, mu_rkv[l], w0[l], a0[l], w2cat, k_k[l], k_a[l],
            r_k[l].reshape(-1), bd)

        ya = _rwkv(r, lw, k, v, kk, bvec)
        yb = _rglru(xb, gb, conv_w[l].reshape(CONV_WIDTH, Bw), conv_b[l],
                    _block_diag(w_rgate[l]).astype(BF16), b_rgate[l],
                    _block_diag(w_igate[l]).astype(BF16), b_igate[l], lam[l])

        n_g, n_e = w_rg.shape[-1], w_re.shape[-1]
        wr = jnp.zeros((D, 128), F32).at[:, :n_g].set(w_rg[l]).at[:, n_g:n_g + n_e].set(w_re[l])
        br = jnp.zeros((128,), F32).at[:n_g].set(b_rg[l]).at[n_g:n_g + n_e].set(b_re[l])
        x1, h2, cw, meta, metat, cnt = _merge(
            x, ya, bonus, g, yb, s1, s2, gt1, sh2, sc2, lnx_g[l], lnx_b[l], bd,
            p_a[l].astype(BF16), p_b[l].astype(BF16), w_out[l].astype(BF16),
            g_ffn[l], wr, br, moe_tile=MOE_TILE)
        counts = cnt[:, :, 0, :N_GROUPS].astype(jnp.int32).reshape(-1)
        x = _moe(counts, x1, h2, cw, meta, metat, w1e[l].astype(BF16), w3e[l].astype(BF16),
                 w2e[l].astype(BF16), gt2, g_final, shf, scf, tm=MOE_TILE)
    return x
```

```python
import functools

import jax
import jax.numpy as jnp
from jax import lax
from jax.experimental import pallas as pl
from jax.experimental.pallas import tpu as pltpu

F32 = jnp.float32
BF16 = jnp.bfloat16

MXU_TILE = 256
HEAD = 64
CHUNK = 64
SUB = 16
QUAD = 4
RMS_EPS = 1e-6
GN_EPS = 64e-5
LRU_C = 8.0
CONV_WIDTH = 4
N_GROUPS = 4
EXPERTS_PER_GROUP = 8
MOE_TILE = 1024
MOE_ROWS = 128
VMEM_LIMIT = 56 * 1024 * 1024


def _cparams(n_axes):
    return pltpu.CompilerParams(dimension_semantics=("arbitrary",) * n_axes,
                                vmem_limit_bytes=VMEM_LIMIT)


def _dot(a, b, dims=(((1,), (0,)), ((), ()))):
    return lax.dot_general(a.astype(BF16), b.astype(BF16), dims,
                           preferred_element_type=F32)


NT = (((1,), (1,)), ((), ()))
TN = (((0,), (0,)), ((), ()))


def _split(a):
    hi = a.astype(BF16)
    lo = (a - hi.astype(F32)).astype(BF16)
    return hi, lo


def _dot_hl(a, b_exact, dims=(((1,), (0,)), ((), ()))):
    hi, lo = _split(a)
    return (lax.dot_general(hi, b_exact, dims, preferred_element_type=F32)
            + lax.dot_general(lo, b_exact, dims, preferred_element_type=F32))


def _head_sums(a, ones_blocks):
    w = ones_blocks.shape[0]
    parts = [_dot_hl(a[:, o:o + w], ones_blocks) for o in range(0, a.shape[1], w)]
    return parts[0] if len(parts) == 1 else jnp.concatenate(parts, axis=1)


def _dot3(a, b, dims=(((1,), (0,)), ((), ()))):
    ah, al = _split(a)
    bh, bl = _split(b)
    d = functools.partial(lax.dot_general, dimension_numbers=dims,
                          preferred_element_type=F32)
    return d(ah, bh) + (d(ah, bl) + d(al, bh))


def _sigmoid(z):
    return 0.5 * jnp.tanh(0.5 * z) + 0.5


def _softplus(z):
    return jnp.maximum(z, 0.0) + jnp.log1p(jnp.exp(-jnp.abs(z)))


def _silu(z):
    return z * _sigmoid(z)


def _gelu_tanh(z):
    return 0.5 * z * (1.0 + jnp.tanh(0.7978845608028654 * (z + 0.044715 * (z * z * z))))


def _ada_kernel(c_ref, w_ref, b_ref, o_ref):
    ca = _silu(c_ref[...])
    o_ref[...] = _dot3(ca, w_ref[...]) + b_ref[...]


def _ada(c8, w, b, tn=1024):
    d, n = w.shape
    return pl.pallas_call(
        _ada_kernel,
        grid=(n // tn,),
        in_specs=[pl.BlockSpec((8, d), lambda j: (0, 0)),
                  pl.BlockSpec((d, tn), lambda j: (0, j)),
                  pl.BlockSpec((1, tn), lambda j: (0, j))],
        out_specs=pl.BlockSpec((8, tn), lambda j: (0, j)),
        out_shape=jax.ShapeDtypeStruct((8, n), F32),
        compiler_params=_cparams(1),
        name="adaln",
    )(c8, w, b.reshape(1, n))


def _inproj_kernel(x_ref, sh_ref, sc_ref, g_ref, wext_ref, mu_ref, w0_ref, a0_ref,
                   w2cat_ref, kk_ref, ka_ref, rk_ref, bd_ref,
                   cw_ref, cb_ref, wr_ref, br_ref, wi_ref, bi_ref, lam_ref,
                   r_out, lw_out, k_out, v_out, kk_out, b_out, g_out, bonus_out,
                   yb_out, s1_out, s2_out, carry_ref, xc_carry, h_carry, *, a_width, b_width,
                   d_model):
    A = a_width
    n_shift = 3 * A + 256
    s = pl.program_id(1)

    @pl.when(s == 0)
    def _():
        carry_ref[...] = jnp.zeros_like(carry_ref)
        xc_carry[...] = jnp.zeros_like(xc_carry)
        h_carry[...] = jnp.zeros_like(h_carry)

    x = x_ref[0]
    tm = x.shape[0]
    ms = jnp.mean(x * x, axis=-1, keepdims=True)
    h = (x * lax.rsqrt(ms + RMS_EPS)) * g_ref[...]
    h = h * (1.0 + sc_ref[0]) + sh_ref[0]
    hb = h.astype(BF16)

    sg = jnp.dot(hb, wext_ref[:, 0:n_shift + 256], preferred_element_type=F32)
    o = n_shift + 256
    xg = jnp.dot(hb, wext_ref[:, o:o + 2 * b_width], preferred_element_type=F32)
    o += 2 * b_width

    def gate_strand():
        step = 2 * MXU_TILE
        for out, base in ((s1_out, o), (s2_out, o + d_model)):
            for c in range(0, d_model, step):
                out[0, :, c:c + step] = _sigmoid(jnp.dot(
                    hb, wext_ref[:, base + c:base + c + step], preferred_element_type=F32))
                yield

    def rwkv_strand():
        yield from _rwkv_heads(sg, n_shift, A, carry_ref, mu_ref, w0_ref, a0_ref, w2cat_ref,
                               kk_ref, ka_ref, rk_ref, bd_ref, r_out, lw_out, k_out, v_out,
                               kk_out, b_out, g_out, bonus_out)

    strands = [gate_strand(), rwkv_strand(),
               _rglru_tile(xg[:, 0:b_width], xg[:, b_width:2 * b_width], cw_ref, cb_ref, wr_ref,
                           br_ref, wi_ref, bi_ref, lam_ref, xc_carry, h_carry, yb_out)]
    while strands:
        strands = [c for c in strands if next(c, "done") != "done"]


def _rwkv_heads(sg, n_shift, A, carry_ref, mu_ref, w0_ref, a0_ref, w2cat_ref, kk_ref, ka_ref,
                rk_ref, bd_ref, r_out, lw_out, k_out, v_out, kk_out, b_out, g_out, bonus_out):
    tm = sg.shape[0]
    cur = sg[:, 0:n_shift]
    row = lax.broadcasted_iota(jnp.int32, (tm, 1), 0)
    prev = jnp.where(row == 0, carry_ref[0:1, :], pltpu.roll(cur, 1, axis=0))
    carry_ref[0:1, :] = cur[tm - 1:tm, :]

    rkv = cur[:, 0:3 * A]
    rkv = rkv + (prev[:, 0:3 * A] - rkv) * mu_ref[...]
    r = rkv[:, 0:A]
    k = rkv[:, A:2 * A]
    v = rkv[:, 2 * A:3 * A]
    r_out[0] = r
    v_out[0] = v
    yield

    pre = sg[:, n_shift:n_shift + 256] + prev[:, 3 * A:3 * A + 256]
    lane = lax.broadcasted_iota(jnp.int32, pre.shape, 1)
    act = jnp.where(lane < 64, jnp.tanh(pre), jnp.where(lane < 128, pre, _sigmoid(pre)))
    low = _dot(act, w2cat_ref[...])
    bd = bd_ref[...]
    kkr = k * kk_ref[...]
    kk_ss = _head_sums(kkr * kkr, bd)
    yield
    w_log = -_softplus(-(w0_ref[...] + low[:, 0:A])) - 0.5
    lw_out[0] = -jnp.exp(w_log)
    g_out[0] = low[:, 2 * A:3 * A]
    yield
    iclr = _sigmoid(a0_ref[...] + low[:, A:2 * A])
    kkn = kkr * lax.rsqrt(kk_ss + 1e-12)
    k2 = k * (1.0 + (iclr - 1.0) * ka_ref[...])
    k_out[0] = k2
    kk_out[0] = kkn
    b_out[0] = kkn * iclr
    yield
    bonus_out[0] = _head_sums(r * k2 * rk_ref[...], bd) * v


def _inproj(x, sh1, sc1, g_mix, wext, mu_rkv, w0, a0, w2cat, k_k, k_a, r_k, bd,
            conv_w, conv_b, wr, br, wi, bi, lam, *, tm=256):
    B, S, D = x.shape
    A = w0.shape[-1]
    Bw = lam.shape[-1]
    row = lambda a: a.reshape(1, -1)
    full = lambda a: pl.BlockSpec(a.shape, lambda b, s: (0,) * a.ndim)
    tok = lambda n: pl.BlockSpec((1, tm, n), lambda b, s: (b, s, 0))
    per_b = pl.BlockSpec((1, 1, D), lambda b, s: (b, 0, 0))
    small = [row(g_mix), wext, row(mu_rkv), row(w0), row(a0), w2cat, row(k_k), row(k_a),
             row(r_k), bd, conv_w, row(conv_b), wr, row(br), wi, row(bi), row(lam)]
    outs = [jax.ShapeDtypeStruct((B, S, A), F32)] * 8 + \
           [jax.ShapeDtypeStruct((B, S, Bw), F32)] + \
           [jax.ShapeDtypeStruct((B, S, D), F32)] * 2
    return pl.pallas_call(
        functools.partial(_inproj_kernel, a_width=A, b_width=Bw, d_model=D),
        grid=(B, S // tm),
        in_specs=[tok(D), per_b, per_b] + [full(a) for a in small],
        out_specs=[tok(A)] * 8 + [tok(Bw)] + [tok(D)] * 2,
        out_shape=outs,
        scratch_shapes=[pltpu.VMEM((8, 3 * A + 256), F32), pltpu.VMEM((8, Bw), F32),
                        pltpu.VMEM((8, Bw), F32)],
        compiler_params=_cparams(2),
        name="inproj",
    )(x, sh1.reshape(B, 1, D), sc1.reshape(B, 1, D), *small)


def _rwkv_kernel(r_ref, lw_ref, k_ref, v_ref, kk_ref, b_ref, tri_ref, bdm_ref, slm_ref,
                 lm_ref, dgm_ref, y_ref, s_ref):
    @pl.when(pl.program_id(0) == 0)
    def _():
        s_ref[...] = jnp.zeros_like(s_ref)

    W = QUAD * HEAD
    n_b, _, n_a = r_ref.shape
    masks = (tri_ref[...], bdm_ref[...], slm_ref[...], lm_ref[...], dgm_ref[...])
    chains = []
    for bi in range(n_b):
        for q in range(n_a // W):
            cols = slice(q * W, (q + 1) * W)
            ins = [ref[bi, :, cols] for ref in (r_ref, lw_ref, k_ref, v_ref, kk_ref, b_ref)]
            chains.append(_rwkv_chunk(*ins, s_ref, bi * (n_a // W) + q, masks,
                                      functools.partial(_store_y, y_ref, bi, cols)))
    while chains:
        chains = [c for c in chains if next(c, "done") != "done"]


def _store_y(y_ref, bi, cols, y):
    y_ref[bi, :, cols] = y


def _rwkv_chunk(r, lw, k, v, kk, b, s_ref, idx, masks, store_y):
    tri, bdm, slm, lm, dgm = masks
    L = CHUNK
    a = -kk

    cl = _dot_hl3(tri, lw)
    clp = cl - lw
    cm = cl[L // 2 - 1:L // 2, :]
    ce = cl[L - 1:L, :]
    at = a * jnp.exp(clp - cm)
    rt = r * jnp.exp(cl - cm)
    e_inv = jnp.exp(cm - cl)
    bt = b * e_inv
    kt = k * e_inv
    e_end = jnp.exp(ce - cl)
    bh = b * e_end
    kh = k * e_end
    g_mid = jnp.exp(cm)
    g_end = jnp.exp(ce)

    tile = lambda m: jnp.concatenate([m] * QUAD, axis=0)
    stack = lambda m: tile(m) * bdm
    a_s = stack(at)
    r_s = stack(rt)
    v_s = stack(v)
    bh_s = stack(bh)
    kh_s = stack(kh)
    bt_t = tile(bt)
    kt_t = tile(kt)
    yield

    a_ab = _dot(a_s, bt_t, NT) * slm
    a_ak = _dot(a_s, kt_t, NT) * slm
    a_rb = _dot(r_s, bt_t, NT) * lm
    a_rk = _dot(r_s, kt_t, NT) * lm
    yield

    d = a_ab * dgm
    e = a_ab - d
    p = d
    d_inv = (lm - slm) + p
    n = 2
    while n < SUB:
        p = _dot(p, p)
        yield
        d_inv = d_inv + _dot(d_inv, p)
        yield
        n *= 2
    f = _dot(d_inv, e)
    yield
    t_inv = d_inv
    fp = f
    n = 1
    terms = []
    while n < L // SUB:
        terms.append(fp)
        n *= 2
        if n < L // SUB:
            fp = _dot(fp, fp)
            yield
    for fp in reversed(terms):
        t_inv = t_inv + _dot(fp, t_inv)
        yield

    s0 = s_ref[idx]
    s_mid = s0 * g_mid
    rhs = _dot(a_s, s_mid, NT) + _dot(a_ak, v_s)
    y_part = _dot(r_s, s_mid, NT) + _dot(a_rk, v_s)
    yield
    u = _dot(t_inv, rhs)
    yield
    y_s = y_part + _dot(a_rb, u)
    s_ref[idx] = s0 * g_end + _dot(u, bh_s, TN) + _dot(v_s, kh_s, TN)
    y = y_s[0:L]
    for q in range(1, QUAD):
        y = y + y_s[q * L:(q + 1) * L]
    store_y(y)


def _dot_hl3(tri, lw):
    hi = lw.astype(BF16)
    r1 = lw - hi.astype(F32)
    mid = r1.astype(BF16)
    lo = (r1 - mid.astype(F32)).astype(BF16)
    d = functools.partial(jnp.dot, preferred_element_type=F32)
    return d(tri, hi) + (d(tri, mid) + d(tri, lo))


def _rwkv(r, lw, k, v, kk, b):
    B, S, A = r.shape
    W = QUAD * HEAD
    L = CHUNK
    idx = jnp.arange(W)
    assert L == HEAD, "stacked rows (head, time) and lanes (head, channel) share one block size"
    same = (idx[:, None] // L) == (idx[None, :] // L)
    ti = idx[:, None] % L
    tj = idx[None, :] % L
    bdm = same.astype(F32)
    slm = (same & (ti > tj)).astype(F32)
    lm = (same & (ti >= tj)).astype(F32)
    dgm = (same & (ti > tj) & (ti // SUB == tj // SUB)).astype(F32)
    tri = (jnp.arange(L)[:, None] >= jnp.arange(L)[None, :]).astype(BF16)
    tok = pl.BlockSpec((B, L, A), lambda c: (0, c, 0))
    full = lambda a: pl.BlockSpec(a.shape, lambda c: (0,) * a.ndim)
    return pl.pallas_call(
        _rwkv_kernel,
        grid=(S // L,),
        in_specs=[tok] * 6 + [full(tri), full(bdm), full(slm), full(lm), full(dgm)],
        out_specs=tok,
        out_shape=jax.ShapeDtypeStruct((B, S, A), F32),
        scratch_shapes=[pltpu.VMEM((B * (A // W), W, W), F32)],
        compiler_params=_cparams(1),
        name="rwkv7",
    )(r, lw, k, v, kk, b, tri, bdm, slm, lm, dgm)


def _wide_block_dot(a, w_ref):
    w = w_ref.shape[-1]
    parts = [_dot(a[:, i * w:(i + 1) * w], w_ref[i]) for i in range(w_ref.shape[0])]
    return parts[0] if len(parts) == 1 else jnp.concatenate(parts, axis=1)


def _rglru_tile(xb, gb, cw_ref, cb_ref, wr_ref, br_ref, wi_ref, bi_ref, lam_ref,
                xc_carry, h_carry, y_out):
    tt, width = xb.shape
    ext = jnp.concatenate([xc_carry[...], xb], axis=0)
    xc_carry[...] = xb[tt - 8:tt, :]
    xc = cb_ref[...] + cw_ref[CONV_WIDTH - 1:CONV_WIDTH, :] * xb
    for d in range(1, CONV_WIDTH):
        xc = xc + cw_ref[CONV_WIDTH - 1 - d:CONV_WIDTH - d, :] * ext[8 - d:8 - d + tt, :]

    pre_r = _wide_block_dot(xc, wr_ref)
    pre_i = _wide_block_dot(xc, wi_ref)
    yield
    gate_r = _sigmoid(pre_r + br_ref[...])
    gate_i = _sigmoid(pre_i + bi_ref[...])
    log_a = -LRU_C * gate_r * _softplus(-lam_ref[...])
    a = jnp.exp(log_a)
    u = jnp.sqrt(-jnp.tanh(log_a) * (a * a + 1.0)) * (gate_i * xc)
    yield

    a3 = a.reshape(tt // 8, 8, width)
    u3 = u.reshape(tt // 8, 8, width)
    sub = lax.broadcasted_iota(jnp.int32, (1, 8, 1), 1)
    d = 1
    while d < 8:
        keep = sub >= d
        a_sh = jnp.where(keep, pltpu.roll(a3, d, axis=1), 1.0)
        u_sh = jnp.where(keep, pltpu.roll(u3, d, axis=1), 0.0)
        u3 = a3 * u_sh + u3
        a3 = a3 * a_sh
        d *= 2
        yield
    gate = _gelu_tanh(gb)
    carry = h_carry[0:1, :]
    groups = []
    for i in range(tt // 8):
        hg = a3[i] * carry + u3[i]
        groups.append(hg)
        carry = hg[7:8, :]
        if i % 8 == 7:
            yield
    h_carry[0:1, :] = carry
    y_out[0] = jnp.concatenate(groups, axis=0) * gate


def _merge_kernel(x_ref, ya_ref, bonus_ref, g_ref, yb_ref, s1_ref, s2_ref, gt1_ref, sh2_ref,
                  sc2_ref, lng_ref, lnb_ref, bd_ref, pa_ref, pb_ref, wout_ref, gffn_ref,
                  wr_ref, br_ref, x1_out, h2_out, logit_out):
    ya = ya_ref[0]
    bd = bd_ref[...]
    inv_n = 1.0 / HEAD
    mean = _head_sums(ya, bd) * inv_n
    yc = ya - mean
    var = _head_sums(yc * yc, bd) * inv_n
    yn = yc * lax.rsqrt(var + GN_EPS) * lng_ref[...] + lnb_ref[...]
    ya2 = (yn + bonus_ref[0]) * g_ref[0]

    merged = s1_ref[0] * _dot(ya2, pa_ref[...]) + s2_ref[0] * _dot(yb_ref[0], pb_ref[...])
    x1 = x_ref[0] + gt1_ref[0] * _dot(merged, wout_ref[...])
    x1_out[0] = x1

    ms = jnp.mean(x1 * x1, axis=-1, keepdims=True)
    h2 = (x1 * lax.rsqrt(ms + RMS_EPS)) * gffn_ref[...]
    h2 = h2 * (1.0 + sc2_ref[0]) + sh2_ref[0]
    h2_out[0] = h2.astype(BF16)

    logit_out[0] = _dot3(h2, wr_ref[...]) + br_ref[...]


def _router_kernel(logit_ref, tri_ref, cw_out, meta_out, metat_out, cnt_out):
    logits = logit_ref[0]
    lane = lax.broadcasted_iota(jnp.int32, logits.shape, 1).astype(F32)
    neg = jnp.float32(-jnp.inf)
    big = jnp.float32(1e9)
    is_g = lane < N_GROUPS
    lg = jnp.where(is_g, logits, neg)
    mg = jnp.max(lg, axis=-1, keepdims=True)
    pg_top = 1.0 / jnp.sum(jnp.exp(lg - mg), axis=-1, keepdims=True)
    g_idx = jnp.min(jnp.where(lg == mg, lane, big), axis=-1, keepdims=True)
    e_lane = lane - N_GROUPS
    in_grp = (e_lane >= g_idx * EXPERTS_PER_GROUP) & (e_lane < (g_idx + 1) * EXPERTS_PER_GROUP)
    le = jnp.where(in_grp, logits, neg)
    me = jnp.max(le, axis=-1, keepdims=True)
    i1 = jnp.min(jnp.where(le == me, lane, big), axis=-1, keepdims=True)
    ee = jnp.exp(le - me)
    se = jnp.sum(ee, axis=-1, keepdims=True)
    rest = jnp.where(lane != i1, le, neg)
    m2 = jnp.max(rest, axis=-1, keepdims=True)
    i2 = jnp.min(jnp.where(rest == m2, lane, big), axis=-1, keepdims=True)
    p1 = 1.0 / se
    p2 = jnp.exp(m2 - me) / se
    den = p1 + p2
    cw = jnp.where(lane == i1, pg_top * (p1 / den), 0.0) + \
        jnp.where(lane == i2, pg_top * (p2 / den), 0.0)
    cw_out[0] = cw

    onehot = jnp.where(lane == g_idx, 1.0, 0.0)
    before = _dot(tri_ref[...], onehot)
    rank = jnp.sum(onehot * before, axis=-1, keepdims=True)
    cnt_out[0, 0] = jnp.broadcast_to(jnp.sum(onehot, axis=0, keepdims=True), cnt_out.shape[2:])
    meta = jnp.where(lane == 0, g_idx, jnp.where(lane == 1, rank, 0.0))
    meta_out[0] = meta
    metat_out[0] = jnp.transpose(meta)[0:8, :]


def _router(logits, *, tm):
    B, S, n = logits.shape
    tri = (jnp.arange(tm)[:, None] > jnp.arange(tm)[None, :]).astype(BF16)
    tok = pl.BlockSpec((1, tm, n), lambda b, s: (b, s, 0))
    return pl.pallas_call(
        _router_kernel,
        grid=(B, S // tm),
        in_specs=[tok, pl.BlockSpec((tm, tm), lambda b, s: (0, 0))],
        out_specs=[tok, tok, pl.BlockSpec((1, 8, tm), lambda b, s: (b, 0, s)),
                   pl.BlockSpec((1, 1, 8, n), lambda b, s: (b, s, 0, 0))],
        out_shape=[jax.ShapeDtypeStruct((B, S, n), F32), jax.ShapeDtypeStruct((B, S, n), F32),
                   jax.ShapeDtypeStruct((B, 8, S), F32),
                   jax.ShapeDtypeStruct((B, S // tm, 8, n), F32)],
        compiler_params=_cparams(2),
        name="router",
    )(logits, tri)


def _merge(x, ya, bonus, g, yb, s1, s2, gt1, sh2, sc2, lnx_g, lnx_b, bd, pa, pb, wout, g_ffn,
           wr, br, *, tm=256):
    B, S, D = x.shape
    A = ya.shape[-1]
    n = wr.shape[1]
    row = lambda a: a.reshape(1, -1)
    full = lambda a: pl.BlockSpec(a.shape, lambda b, s: (0,) * a.ndim)
    tok = lambda n: pl.BlockSpec((1, tm, n), lambda b, s: (b, s, 0))
    per_b = pl.BlockSpec((1, 1, D), lambda b, s: (b, 0, 0))
    small = [row(lnx_g), row(lnx_b), bd, pa, pb, wout, row(g_ffn), wr, row(br)]
    return pl.pallas_call(
        _merge_kernel,
        grid=(B, S // tm),
        in_specs=[tok(D), tok(A), tok(A), tok(A), tok(A), tok(D), tok(D), per_b, per_b, per_b]
        + [full(a) for a in small],
        out_specs=[tok(D), tok(D), tok(n)],
        out_shape=[jax.ShapeDtypeStruct((B, S, D), F32),
                   jax.ShapeDtypeStruct((B, S, D), BF16),
                   jax.ShapeDtypeStruct((B, S, n), F32)],
        compiler_params=_cparams(2),
        name="merge",
    )(x, ya, bonus, g, yb, s1, s2, gt1.reshape(B, 1, D), sh2.reshape(B, 1, D),
      sc2.reshape(B, 1, D), *small)


def _moe_kernel(cnt_smem, x1_ref, h2_ref, cw_ref, meta_ref, metat_ref, w1_ref, w3_ref, w2_ref,
                gt2_ref, gf_ref, shf_ref, scf_ref, o_ref, acc_ref, ybuf_ref):
    b = pl.program_id(0)
    i = pl.program_id(1)
    g = pl.program_id(2)
    n_g = pl.num_programs(2)
    R = MOE_ROWS

    @pl.when(g == 0)
    def _():
        acc_ref[...] = jnp.zeros_like(acc_ref)

    gf = g.astype(F32)
    n_tok = cnt_smem[(b * pl.num_programs(1) + i) * n_g + g]
    grp_row = metat_ref[0, 0:1, :]
    rank_row = metat_ref[0, 1:2, :]
    meta = meta_ref[0]
    grp_col = meta[:, 0:1]
    rank_col = meta[:, 1:2]
    sel_row = jnp.where(grp_row == gf, rank_row, -1.0)
    sel_col = jnp.where(grp_col == gf, rank_col, -1.0)
    cw = cw_ref[0]
    cw_hi, cw_lo = _split(cw)
    lane = lax.broadcasted_iota(jnp.int32, (R, cw.shape[1]), 1)
    row_ids = lax.broadcasted_iota(jnp.int32, (R, 1), 0).astype(F32)
    lane_ids = lax.broadcasted_iota(jnp.int32, (1, 2 * R), 1).astype(F32)

    def block(rb, carry):
        r0 = (rb * R).astype(F32)
        p = jnp.where(sel_row == row_ids + r0, 1.0, 0.0).astype(BF16)
        xg = jnp.dot(p, h2_ref[0], preferred_element_type=F32).astype(BF16)
        cwg = jnp.dot(p, cw_hi, preferred_element_type=F32) + \
            jnp.dot(p, cw_lo, preferred_element_type=F32)
        hs = []
        for e in range(EXPERTS_PER_GROUP):
            he = _silu(jnp.dot(xg, w1_ref[e], preferred_element_type=F32)) * \
                jnp.dot(xg, w3_ref[e], preferred_element_type=F32)
            col = jnp.sum(jnp.where(lane == N_GROUPS + g * EXPERTS_PER_GROUP + e, cwg, 0.0),
                          axis=-1, keepdims=True)
            hs.append((col * he).astype(BF16))
        y = jnp.dot(jnp.concatenate(hs, axis=1), w2_ref[...].reshape(-1, w2_ref.shape[-1]),
                    preferred_element_type=F32)
        slot = rb % 2
        ybuf_ref[pl.ds(pl.multiple_of(slot * R, R), R), :] = y.astype(BF16)
        last = rb == n_blk - 1

        @pl.when(jnp.logical_and(last, slot == 0))
        def _():
            ybuf_ref[R:2 * R, :] = jnp.zeros((R, ybuf_ref.shape[1]), BF16)

        @pl.when(jnp.logical_or(last, slot == 1))
        def _():
            r0p = ((rb - slot) * R).astype(F32)
            pt = jnp.where(sel_col == lane_ids + r0p, 1.0, 0.0).astype(BF16)
            acc_ref[...] += jnp.dot(pt, ybuf_ref[...], preferred_element_type=F32)
        return carry

    n_blk = (n_tok + R - 1) // R
    lax.fori_loop(0, n_blk, block, 0)

    @pl.when(g == n_g - 1)
    def _():
        x2 = x1_ref[0] + gt2_ref[0] * acc_ref[...]
        ms = jnp.mean(x2 * x2, axis=-1, keepdims=True)
        y = (x2 * lax.rsqrt(ms + RMS_EPS)) * gf_ref[...]
        o_ref[0] = y * (1.0 + scf_ref[0]) + shf_ref[0]


def _moe(counts, x1, h2, cw, meta, metat, w1, w3, w2, gt2, g_final, shf, scf, *, tm):
    B, S, D = x1.shape
    E, _, F = w1.shape
    G = EXPERTS_PER_GROUP
    tok = lambda n: pl.BlockSpec((1, tm, n), lambda b, s, g, c: (b, s, 0))
    per_b = pl.BlockSpec((1, 1, D), lambda b, s, g, c: (b, 0, 0))
    grid_spec = pltpu.PrefetchScalarGridSpec(
        num_scalar_prefetch=1,
        grid=(B, S // tm, E // G),
        in_specs=[tok(D), tok(D), tok(128), tok(128),
                  pl.BlockSpec((1, 8, tm), lambda b, s, g, c: (b, 0, s)),
                  pl.BlockSpec((G, D, F), lambda b, s, g, c: (g, 0, 0)),
                  pl.BlockSpec((G, D, F), lambda b, s, g, c: (g, 0, 0)),
                  pl.BlockSpec((G, F, D), lambda b, s, g, c: (g, 0, 0)),
                  per_b, pl.BlockSpec((1, D), lambda b, s, g, c: (0, 0)), per_b, per_b],
        out_specs=tok(D),
        scratch_shapes=[pltpu.VMEM((tm, D), F32), pltpu.VMEM((2 * MOE_ROWS, D), BF16)])
    return pl.pallas_call(
        _moe_kernel,
        grid_spec=grid_spec,
        out_shape=jax.ShapeDtypeStruct((B, S, D), F32),
        compiler_params=_cparams(3),
        name="moe",
    )(counts, x1, h2, cw, meta, metat, w1, w3, w2, gt2.reshape(B, 1, D), g_final.reshape(1, D),
      shf.reshape(B, 1, D), scf.reshape(B, 1, D))


def _block_diag(w):
    G, n, _ = w.shape
    eye = jnp.eye(G, dtype=w.dtype)
    return (eye[:, None, :, None] * w[:, :, None, :]).reshape(G * n, G * n)


def kernel(x, c, w_ada, b_ada, g_mix, w_in, mu_rkv, mu_wag, w0, w1, w2, a0, a1, a2, g1, g2, k_k, k_a, r_k, lnx_g, lnx_b, conv_w, conv_b, w_rgate, b_rgate, w_igate, b_igate, lam, p_a, p_b, w_out, g_ffn, w_rg, b_rg, w_re, b_re, w1e, w3e, w2e, g_final, w_ada_f, b_ada_f):
    B, S, D = x.shape
    depth = w_ada.shape[0]
    assert depth == 1, "the final norm is fused into the last MoE call; one layer supported"
    A = w0.shape[-1]
    Bw = lam.shape[-1]

    c8 = jnp.zeros((8, D), F32).at[:B].set(c)
    modf = _ada(c8, w_ada_f, b_ada_f)[:B]
    shf, scf = modf[:, :D], modf[:, D:]

    head_ids = jnp.arange(QUAD * HEAD) // HEAD
    bd = (head_ids[:, None] == head_ids[None, :]).astype(BF16)

    for l in range(depth):
        mod = _ada(c8, w_ada[l], b_ada[l])[:B]
        sh1, sc1, gt1, sh2, sc2, gt2 = (mod[:, i * D:(i + 1) * D] for i in range(6))

        lowrank = jnp.concatenate([w1[l], a1[l], g1[l]], axis=1)
        mu_cols = jnp.concatenate([jnp.broadcast_to(mu_wag[l, 0][:, None], w1[l].shape),
                                   jnp.broadcast_to(mu_wag[l, 1][:, None], a1[l].shape),
                                   jnp.broadcast_to(mu_wag[l, 2][:, None], g1[l].shape)], axis=1)
        o1 = 3 * A
        wext = jnp.concatenate([w_in[l][:, :o1], mu_cols * lowrank, (1.0 - mu_cols) * lowrank,
                                w_in[l][:, o1:]], axis=1).astype(BF16)
        ra, rb = w2.shape[1], g2.shape[1]
        w2cat = jnp.zeros((2 * ra + rb, 3 * A), F32)
        w2cat = w2cat.at[0:ra, 0:A].set(w2[l]).at[ra:2 * ra, A:2 * A].set(a2[l])
        w2cat = w2cat.at[2 * ra:, 2 * A:].set(g2[l]).astype(BF16)

        per_tile = QUAD * HEAD // w_rgate.shape[-1]
        wide = lambda w: jax.vmap(_block_diag)(
            w.reshape(-1, per_tile, *w.shape[1:])).astype(BF16)
        (r, lw, k, v, kk, bvec, g, bonus, yb, s1, s2) = _inproj(
            x, sh1, sc1, g_mix[l], wext, mu_rkv[l], w0[l], a0[l], w2cat, k_k[l], k_a[l],
            r_k[l].reshape(-1), bd, conv_w[l].reshape(CONV_WIDTH, Bw), conv_b[l],
            wide(w_rgate[l]), b_rgate[l], wide(w_igate[l]), b_igate[l], lam[l])

        ya = _rwkv(r, lw, k, v, kk, bvec)

        n_g, n_e = w_rg.shape[-1], w_re.shape[-1]
        wr = jnp.zeros((D, 128), F32).at[:, :n_g].set(w_rg[l]).at[:, n_g:n_g + n_e].set(w_re[l])
        br = jnp.zeros((128,), F32).at[:n_g].set(b_rg[l]).at[n_g:n_g + n_e].set(b_re[l])
        x1, h2, logits = _merge(
            x, ya, bonus, g, yb, s1, s2, gt1, sh2, sc2, lnx_g[l], lnx_b[l], bd,
            p_a[l].astype(BF16), p_b[l].astype(BF16), w_out[l].astype(BF16),
            g_ffn[l], wr, br)
        cw, meta, metat, cnt = _router(logits, tm=MOE_TILE)
        counts = cnt[:, :, 0, :N_GROUPS].astype(jnp.int32).reshape(-1)
        x = _moe(counts, x1, h2, cw, meta, metat, w1e[l].astype(BF16), w3e[l].astype(BF16),
                 w2e[l].astype(BF16), gt2, g_final, shf, scf, tm=MOE_TILE)
    return x
```

```python
import functools

import jax
import jax.numpy as jnp
from jax import lax
from jax.experimental import pallas as pl
from jax.experimental.pallas import tpu as pltpu

F32 = jnp.float32
BF16 = jnp.bfloat16

MXU_TILE = 256
HEAD = 64
CHUNK = 64
SUB = 16
QUAD = 4
RMS_EPS = 1e-6
GN_EPS = 64e-5
LRU_C = 8.0
CONV_WIDTH = 4
N_GROUPS = 4
EXPERTS_PER_GROUP = 8
N_EXPERTS = N_GROUPS * EXPERTS_PER_GROUP
MOE_TILE = 1024
MOE_SUB = 256
MOE_ALIGN = 16
MOE_WINDOW = 32
MOE_ITEMS = 4
MOE_CAP = 2 * MOE_SUB + N_EXPERTS * (MOE_ALIGN - 1) + MOE_WINDOW
VMEM_LIMIT = 56 * 1024 * 1024


def _cparams(n_axes):
    return pltpu.CompilerParams(dimension_semantics=("arbitrary",) * n_axes,
                                vmem_limit_bytes=VMEM_LIMIT)


def _dot(a, b, dims=(((1,), (0,)), ((), ()))):
    return lax.dot_general(a.astype(BF16), b.astype(BF16), dims,
                           preferred_element_type=F32)


NT = (((1,), (1,)), ((), ()))
TN = (((0,), (0,)), ((), ()))


def _split(a):
    hi = a.astype(BF16)
    lo = (a - hi.astype(F32)).astype(BF16)
    return hi, lo


def _dot_hl(a, b_exact, dims=(((1,), (0,)), ((), ()))):
    hi, lo = _split(a)
    return (lax.dot_general(hi, b_exact, dims, preferred_element_type=F32)
            + lax.dot_general(lo, b_exact, dims, preferred_element_type=F32))


def _head_sums(a, ones_blocks):
    w = ones_blocks.shape[0]
    parts = [_dot_hl(a[:, o:o + w], ones_blocks) for o in range(0, a.shape[1], w)]
    return parts[0] if len(parts) == 1 else jnp.concatenate(parts, axis=1)


def _dot3(a, b, dims=(((1,), (0,)), ((), ()))):
    ah, al = _split(a)
    bh, bl = _split(b)
    d = functools.partial(lax.dot_general, dimension_numbers=dims,
                          preferred_element_type=F32)
    return d(ah, bh) + (d(ah, bl) + d(al, bh))


def _sigmoid(z):
    return 0.5 * jnp.tanh(0.5 * z) + 0.5


def _softplus(z):
    return jnp.maximum(z, 0.0) + jnp.log1p(jnp.exp(-jnp.abs(z)))


def _silu(z):
    return z * _sigmoid(z)


def _gelu_tanh(z):
    return 0.5 * z * (1.0 + jnp.tanh(0.7978845608028654 * (z + 0.044715 * (z * z * z))))


def _ada_kernel(c_ref, w_ref, b_ref, o_ref):
    ca = _silu(c_ref[...])
    o_ref[...] = _dot3(ca, w_ref[...]) + b_ref[...]


def _ada(c8, w, b, tn=1024):
    d, n = w.shape
    return pl.pallas_call(
        _ada_kernel,
        grid=(n // tn,),
        in_specs=[pl.BlockSpec((8, d), lambda j: (0, 0)),
                  pl.BlockSpec((d, tn), lambda j: (0, j)),
                  pl.BlockSpec((1, tn), lambda j: (0, j))],
        out_specs=pl.BlockSpec((8, tn), lambda j: (0, j)),
        out_shape=jax.ShapeDtypeStruct((8, n), F32),
        compiler_params=_cparams(1),
        name="adaln",
    )(c8, w, b.reshape(1, n))


def _inproj_kernel(x_ref, sh_ref, sc_ref, g_ref, wext_ref, mu_ref, w0_ref, a0_ref,
                   w2cat_ref, kk_ref, ka_ref, rk_ref, bd_ref,
                   cw_ref, cb_ref, wr_ref, br_ref, wi_ref, bi_ref, lam_ref,
                   r_out, lw_out, k_out, v_out, kk_out, b_out, g_out, bonus_out,
                   yb_out, s1_out, s2_out, carry_ref, xc_carry, h_carry, *, a_width, b_width,
                   d_model):
    A = a_width
    n_shift = 3 * A + 256
    s = pl.program_id(1)

    @pl.when(s == 0)
    def _():
        carry_ref[...] = jnp.zeros_like(carry_ref)
        xc_carry[...] = jnp.zeros_like(xc_carry)
        h_carry[...] = jnp.zeros_like(h_carry)

    x = x_ref[0]
    tm = x.shape[0]
    ms = jnp.mean(x * x, axis=-1, keepdims=True)
    h = (x * lax.rsqrt(ms + RMS_EPS)) * g_ref[...]
    h = h * (1.0 + sc_ref[0]) + sh_ref[0]
    hb = h.astype(BF16)

    sg = jnp.dot(hb, wext_ref[:, 0:n_shift + 256], preferred_element_type=F32)
    o = n_shift + 256
    xg = jnp.dot(hb, wext_ref[:, o:o + 2 * b_width], preferred_element_type=F32)
    o += 2 * b_width

    def gate_strand():
        step = 2 * MXU_TILE
        for out, base in ((s1_out, o), (s2_out, o + d_model)):
            for c in range(0, d_model, step):
                out[0, :, c:c + step] = _sigmoid(jnp.dot(
                    hb, wext_ref[:, base + c:base + c + step], preferred_element_type=F32))
                yield

    def rwkv_strand():
        yield from _rwkv_heads(sg, n_shift, A, carry_ref, mu_ref, w0_ref, a0_ref, w2cat_ref,
                               kk_ref, ka_ref, rk_ref, bd_ref, r_out, lw_out, k_out, v_out,
                               kk_out, b_out, g_out, bonus_out)

    strands = [gate_strand(), rwkv_strand(),
               _rglru_tile(xg[:, 0:b_width], xg[:, b_width:2 * b_width], cw_ref, cb_ref, wr_ref,
                           br_ref, wi_ref, bi_ref, lam_ref, xc_carry, h_carry, yb_out)]
    while strands:
        strands = [c for c in strands if next(c, "done") != "done"]


def _rwkv_heads(sg, n_shift, A, carry_ref, mu_ref, w0_ref, a0_ref, w2cat_ref, kk_ref, ka_ref,
                rk_ref, bd_ref, r_out, lw_out, k_out, v_out, kk_out, b_out, g_out, bonus_out):
    tm = sg.shape[0]
    cur = sg[:, 0:n_shift]
    row = lax.broadcasted_iota(jnp.int32, (tm, 1), 0)
    prev = jnp.where(row == 0, carry_ref[0:1, :], pltpu.roll(cur, 1, axis=0))
    carry_ref[0:1, :] = cur[tm - 1:tm, :]

    rkv = cur[:, 0:3 * A]
    rkv = rkv + (prev[:, 0:3 * A] - rkv) * mu_ref[...]
    r = rkv[:, 0:A]
    k = rkv[:, A:2 * A]
    v = rkv[:, 2 * A:3 * A]
    r_out[0] = r
    v_out[0] = v
    yield

    pre = sg[:, n_shift:n_shift + 256] + prev[:, 3 * A:3 * A + 256]
    lane = lax.broadcasted_iota(jnp.int32, pre.shape, 1)
    act = jnp.where(lane < 64, jnp.tanh(pre), jnp.where(lane < 128, pre, _sigmoid(pre)))
    low = _dot(act, w2cat_ref[...])
    bd = bd_ref[...]
    kkr = k * kk_ref[...]
    kk_ss = _head_sums(kkr * kkr, bd)
    yield
    w_log = -_softplus(-(w0_ref[...] + low[:, 0:A])) - 0.5
    lw_out[0] = -jnp.exp(w_log)
    g_out[0] = low[:, 2 * A:3 * A]
    yield
    iclr = _sigmoid(a0_ref[...] + low[:, A:2 * A])
    kkn = kkr * lax.rsqrt(kk_ss + 1e-12)
    k2 = k * (1.0 + (iclr - 1.0) * ka_ref[...])
    k_out[0] = k2
    kk_out[0] = kkn
    b_out[0] = kkn * iclr
    yield
    bonus_out[0] = _head_sums(r * k2 * rk_ref[...], bd) * v


def _inproj(x, sh1, sc1, g_mix, wext, mu_rkv, w0, a0, w2cat, k_k, k_a, r_k, bd,
            conv_w, conv_b, wr, br, wi, bi, lam, *, tm=256):
    B, S, D = x.shape
    A = w0.shape[-1]
    Bw = lam.shape[-1]
    row = lambda a: a.reshape(1, -1)
    full = lambda a: pl.BlockSpec(a.shape, lambda b, s: (0,) * a.ndim)
    tok = lambda n: pl.BlockSpec((1, tm, n), lambda b, s: (b, s, 0))
    per_b = pl.BlockSpec((1, 1, D), lambda b, s: (b, 0, 0))
    small = [row(g_mix), wext, row(mu_rkv), row(w0), row(a0), w2cat, row(k_k), row(k_a),
             row(r_k), bd, conv_w, row(conv_b), wr, row(br), wi, row(bi), row(lam)]
    outs = [jax.ShapeDtypeStruct((B, S, A), F32)] * 8 + \
           [jax.ShapeDtypeStruct((B, S, Bw), F32)] + \
           [jax.ShapeDtypeStruct((B, S, D), F32)] * 2
    return pl.pallas_call(
        functools.partial(_inproj_kernel, a_width=A, b_width=Bw, d_model=D),
        grid=(B, S // tm),
        in_specs=[tok(D), per_b, per_b] + [full(a) for a in small],
        out_specs=[tok(A)] * 8 + [tok(Bw)] + [tok(D)] * 2,
        out_shape=outs,
        scratch_shapes=[pltpu.VMEM((8, 3 * A + 256), F32), pltpu.VMEM((8, Bw), F32),
                        pltpu.VMEM((8, Bw), F32)],
        compiler_params=_cparams(2),
        name="inproj",
    )(x, sh1.reshape(B, 1, D), sc1.reshape(B, 1, D), *small)


def _rwkv_kernel(r_ref, lw_ref, k_ref, v_ref, kk_ref, b_ref, tri_ref, bdm_ref, slm_ref,
                 lm_ref, dgm_ref, y_ref, s_ref):
    @pl.when(pl.program_id(0) == 0)
    def _():
        s_ref[...] = jnp.zeros_like(s_ref)

    W = QUAD * HEAD
    n_b, _, n_a = r_ref.shape
    masks = (tri_ref[...], bdm_ref[...], slm_ref[...], lm_ref[...], dgm_ref[...])
    chains = []
    for bi in range(n_b):
        for q in range(n_a // W):
            cols = slice(q * W, (q + 1) * W)
            ins = [ref[bi, :, cols] for ref in (r_ref, lw_ref, k_ref, v_ref, kk_ref, b_ref)]
            chains.append(_rwkv_chunk(*ins, s_ref, bi * (n_a // W) + q, masks,
                                      functools.partial(_store_y, y_ref, bi, cols)))
    while chains:
        chains = [c for c in chains if next(c, "done") != "done"]


def _store_y(y_ref, bi, cols, y):
    y_ref[bi, :, cols] = y


def _rwkv_chunk(r, lw, k, v, kk, b, s_ref, idx, masks, store_y):
    tri, bdm, slm, lm, dgm = masks
    L = CHUNK
    a = -kk

    cl = _dot_hl3(tri, lw)
    clp = cl - lw
    cm = cl[L // 2 - 1:L // 2, :]
    ce = cl[L - 1:L, :]
    at = a * jnp.exp(clp - cm)
    rt = r * jnp.exp(cl - cm)
    e_inv = jnp.exp(cm - cl)
    bt = b * e_inv
    kt = k * e_inv
    e_end = jnp.exp(ce - cl)
    bh = b * e_end
    kh = k * e_end
    g_mid = jnp.exp(cm)
    g_end = jnp.exp(ce)

    tile = lambda m: jnp.concatenate([m] * QUAD, axis=0)
    stack = lambda m: tile(m) * bdm
    a_s = stack(at)
    r_s = stack(rt)
    v_s = stack(v)
    bh_s = stack(bh)
    kh_s = stack(kh)
    bt_t = tile(bt)
    kt_t = tile(kt)
    yield

    a_ab = _dot(a_s, bt_t, NT) * slm
    a_ak = _dot(a_s, kt_t, NT) * slm
    a_rb = _dot(r_s, bt_t, NT) * lm
    a_rk = _dot(r_s, kt_t, NT) * lm
    yield

    d = a_ab * dgm
    e = a_ab - d
    p = d
    d_inv = (lm - slm) + p
    n = 2
    while n < SUB:
        p = _dot(p, p)
        yield
        d_inv = d_inv + _dot(d_inv, p)
        yield
        n *= 2
    f = _dot(d_inv, e)
    yield
    t_inv = d_inv
    fp = f
    n = 1
    terms = []
    while n < L // SUB:
        terms.append(fp)
        n *= 2
        if n < L // SUB:
            fp = _dot(fp, fp)
            yield
    for fp in reversed(terms):
        t_inv = t_inv + _dot(fp, t_inv)
        yield

    s0 = s_ref[idx]
    s_mid = s0 * g_mid
    rhs = _dot(a_s, s_mid, NT) + _dot(a_ak, v_s)
    y_part = _dot(r_s, s_mid, NT) + _dot(a_rk, v_s)
    yield
    u = _dot(t_inv, rhs)
    yield
    y_s = y_part + _dot(a_rb, u)
    s_ref[idx] = s0 * g_end + _dot(u, bh_s, TN) + _dot(v_s, kh_s, TN)
    y = y_s[0:L]
    for q in range(1, QUAD):
        y = y + y_s[q * L:(q + 1) * L]
    store_y(y)


def _dot_hl3(tri, lw):
    hi = lw.astype(BF16)
    r1 = lw - hi.astype(F32)
    mid = r1.astype(BF16)
    lo = (r1 - mid.astype(F32)).astype(BF16)
    d = functools.partial(jnp.dot, preferred_element_type=F32)
    return d(tri, hi) + (d(tri, mid) + d(tri, lo))


def _rwkv(r, lw, k, v, kk, b):
    B, S, A = r.shape
    W = QUAD * HEAD
    L = CHUNK
    idx = jnp.arange(W)
    assert L == HEAD, "stacked rows (head, time) and lanes (head, channel) share one block size"
    same = (idx[:, None] // L) == (idx[None, :] // L)
    ti = idx[:, None] % L
    tj = idx[None, :] % L
    bdm = same.astype(F32)
    slm = (same & (ti > tj)).astype(F32)
    lm = (same & (ti >= tj)).astype(F32)
    dgm = (same & (ti > tj) & (ti // SUB == tj // SUB)).astype(F32)
    tri = (jnp.arange(L)[:, None] >= jnp.arange(L)[None, :]).astype(BF16)
    tok = pl.BlockSpec((B, L, A), lambda c: (0, c, 0))
    full = lambda a: pl.BlockSpec(a.shape, lambda c: (0,) * a.ndim)
    return pl.pallas_call(
        _rwkv_kernel,
        grid=(S // L,),
        in_specs=[tok] * 6 + [full(tri), full(bdm), full(slm), full(lm), full(dgm)],
        out_specs=tok,
        out_shape=jax.ShapeDtypeStruct((B, S, A), F32),
        scratch_shapes=[pltpu.VMEM((B * (A // W), W, W), F32)],
        compiler_params=_cparams(1),
        name="rwkv7",
    )(r, lw, k, v, kk, b, tri, bdm, slm, lm, dgm)


def _wide_block_dot(a, w_ref):
    w = w_ref.shape[-1]
    parts = [_dot(a[:, i * w:(i + 1) * w], w_ref[i]) for i in range(w_ref.shape[0])]
    return parts[0] if len(parts) == 1 else jnp.concatenate(parts, axis=1)


def _rglru_tile(xb, gb, cw_ref, cb_ref, wr_ref, br_ref, wi_ref, bi_ref, lam_ref,
                xc_carry, h_carry, y_out):
    tt, width = xb.shape
    ext = jnp.concatenate([xc_carry[...], xb], axis=0)
    xc_carry[...] = xb[tt - 8:tt, :]
    xc = cb_ref[...] + cw_ref[CONV_WIDTH - 1:CONV_WIDTH, :] * xb
    for d in range(1, CONV_WIDTH):
        xc = xc + cw_ref[CONV_WIDTH - 1 - d:CONV_WIDTH - d, :] * ext[8 - d:8 - d + tt, :]

    pre_r = _wide_block_dot(xc, wr_ref)
    pre_i = _wide_block_dot(xc, wi_ref)
    yield
    gate_r = _sigmoid(pre_r + br_ref[...])
    gate_i = _sigmoid(pre_i + bi_ref[...])
    log_a = -LRU_C * gate_r * _softplus(-lam_ref[...])
    a = jnp.exp(log_a)
    u = jnp.sqrt(-jnp.tanh(log_a) * (a * a + 1.0)) * (gate_i * xc)
    yield

    a3 = a.reshape(tt // 8, 8, width)
    u3 = u.reshape(tt // 8, 8, width)
    sub = lax.broadcasted_iota(jnp.int32, (1, 8, 1), 1)
    d = 1
    while d < 8:
        keep = sub >= d
        a_sh = jnp.where(keep, pltpu.roll(a3, d, axis=1), 1.0)
        u_sh = jnp.where(keep, pltpu.roll(u3, d, axis=1), 0.0)
        u3 = a3 * u_sh + u3
        a3 = a3 * a_sh
        d *= 2
        yield
    gate = _gelu_tanh(gb)
    carry = h_carry[0:1, :]
    groups = []
    for i in range(tt // 8):
        hg = a3[i] * carry + u3[i]
        groups.append(hg)
        carry = hg[7:8, :]
        if i % 8 == 7:
            yield
    h_carry[0:1, :] = carry
    y_out[0] = jnp.concatenate(groups, axis=0) * gate


def _merge_kernel(x_ref, ya_ref, bonus_ref, g_ref, yb_ref, s1_ref, s2_ref, gt1_ref, sh2_ref,
                  sc2_ref, lng_ref, lnb_ref, bd_ref, pa_ref, pb_ref, wout_ref, gffn_ref,
                  wr_ref, br_ref, x1_out, h2_out, logit_out):
    ya = ya_ref[0]
    bd = bd_ref[...]
    inv_n = 1.0 / HEAD
    mean = _head_sums(ya, bd) * inv_n
    yc = ya - mean
    var = _head_sums(yc * yc, bd) * inv_n
    yn = yc * lax.rsqrt(var + GN_EPS) * lng_ref[...] + lnb_ref[...]
    ya2 = (yn + bonus_ref[0]) * g_ref[0]

    merged = s1_ref[0] * _dot(ya2, pa_ref[...]) + s2_ref[0] * _dot(yb_ref[0], pb_ref[...])
    x1 = x_ref[0] + gt1_ref[0] * _dot(merged, wout_ref[...])
    x1_out[0] = x1

    ms = jnp.mean(x1 * x1, axis=-1, keepdims=True)
    h2 = (x1 * lax.rsqrt(ms + RMS_EPS)) * gffn_ref[...]
    h2 = h2 * (1.0 + sc2_ref[0]) + sh2_ref[0]
    h2_out[0] = h2.astype(BF16)

    logit_out[0] = _dot3(h2, wr_ref[...]) + br_ref[...]


def _router_kernel(logit_ref, tri_ref, utri_ref, meta_out, metat_out, seg_out):
    logits = logit_ref[0]
    lane = lax.broadcasted_iota(jnp.int32, logits.shape, 1).astype(F32)
    neg = jnp.float32(-jnp.inf)
    big = jnp.float32(1e9)
    is_g = lane < N_GROUPS
    lg = jnp.where(is_g, logits, neg)
    mg = jnp.max(lg, axis=-1, keepdims=True)
    pg_top = 1.0 / jnp.sum(jnp.exp(lg - mg), axis=-1, keepdims=True)
    g_idx = jnp.min(jnp.where(lg == mg, lane, big), axis=-1, keepdims=True)
    e_lane = lane - N_GROUPS
    in_grp = (e_lane >= g_idx * EXPERTS_PER_GROUP) & (e_lane < (g_idx + 1) * EXPERTS_PER_GROUP)
    le = jnp.where(in_grp, logits, neg)
    me = jnp.max(le, axis=-1, keepdims=True)
    i1 = jnp.min(jnp.where(le == me, lane, big), axis=-1, keepdims=True)
    ee = jnp.exp(le - me)
    se = jnp.sum(ee, axis=-1, keepdims=True)
    rest = jnp.where(lane != i1, le, neg)
    m2 = jnp.max(rest, axis=-1, keepdims=True)
    i2 = jnp.min(jnp.where(rest == m2, lane, big), axis=-1, keepdims=True)
    p1 = 1.0 / se
    p2 = jnp.exp(m2 - me) / se
    den = p1 + p2
    wt1 = pg_top * (p1 / den)
    wt2 = pg_top * (p2 / den)

    hot1 = jnp.where(lane == i1, 1.0, 0.0)
    hot2 = jnp.where(lane == i2, 1.0, 0.0)
    both = hot1 + hot2
    n_sub = logits.shape[0] // MOE_SUB
    pos1, pos2, seg_rows = [], [], []
    for s in range(n_sub):
        rows = slice(s * MOE_SUB, (s + 1) * MOE_SUB)
        before = _dot(tri_ref[...], both[rows])
        count = jnp.sum(both[rows], axis=0, keepdims=True)
        padded = jnp.floor((count + (MOE_ALIGN - 1)) * (1.0 / MOE_ALIGN)) * MOE_ALIGN
        start = _dot_hl(jnp.broadcast_to(padded, (8, padded.shape[1])), utri_ref[...])[0:1] \
            + float(s * MOE_CAP)
        slot = before + start
        pos1.append(jnp.sum(hot1[rows] * slot, axis=-1, keepdims=True))
        pos2.append(jnp.sum(hot2[rows] * slot, axis=-1, keepdims=True))
        seg_rows.append((start, count))
    pos1 = jnp.concatenate(pos1, axis=0)
    pos2 = jnp.concatenate(pos2, axis=0)
    meta = jnp.where(lane == 0, pos1, jnp.where(lane == 1, pos2,
                     jnp.where(lane == 2, wt1, jnp.where(lane == 3, wt2, 0.0))))
    meta_out[0] = meta
    metat_out[0] = jnp.transpose(meta)[0:8, :]
    seg = [st for st, _ in seg_rows] + [ct for _, ct in seg_rows]
    seg_out[0, 0] = jnp.concatenate(seg, axis=0)


def _router(logits, *, tm):
    B, S, n = logits.shape
    assert 2 * (tm // MOE_SUB) == 8, "segment table holds one start and one count row per sub-tile"
    tri = (jnp.arange(MOE_SUB)[:, None] > jnp.arange(MOE_SUB)[None, :]).astype(BF16)
    utri = (jnp.arange(n)[:, None] < jnp.arange(n)[None, :]).astype(BF16)
    tok = pl.BlockSpec((1, tm, n), lambda b, s: (b, s, 0))
    full = lambda a: pl.BlockSpec(a.shape, lambda b, s: (0,) * a.ndim)
    return pl.pallas_call(
        _router_kernel,
        grid=(B, S // tm),
        in_specs=[tok, full(tri), full(utri)],
        out_specs=[tok, pl.BlockSpec((1, 8, tm), lambda b, s: (b, 0, s)),
                   pl.BlockSpec((1, 1, 8, n), lambda b, s: (b, s, 0, 0))],
        out_shape=[jax.ShapeDtypeStruct((B, S, n), F32),
                   jax.ShapeDtypeStruct((B, 8, S), F32),
                   jax.ShapeDtypeStruct((B, S // tm, 8, n), F32)],
        compiler_params=_cparams(2),
        name="router",
    )(logits, tri, utri)


def _merge(x, ya, bonus, g, yb, s1, s2, gt1, sh2, sc2, lnx_g, lnx_b, bd, pa, pb, wout, g_ffn,
           wr, br, *, tm=256):
    B, S, D = x.shape
    A = ya.shape[-1]
    n = wr.shape[1]
    row = lambda a: a.reshape(1, -1)
    full = lambda a: pl.BlockSpec(a.shape, lambda b, s: (0,) * a.ndim)
    tok = lambda n: pl.BlockSpec((1, tm, n), lambda b, s: (b, s, 0))
    per_b = pl.BlockSpec((1, 1, D), lambda b, s: (b, 0, 0))
    small = [row(lnx_g), row(lnx_b), bd, pa, pb, wout, row(g_ffn), wr, row(br)]
    return pl.pallas_call(
        _merge_kernel,
        grid=(B, S // tm),
        in_specs=[tok(D), tok(A), tok(A), tok(A), tok(A), tok(D), tok(D), per_b, per_b, per_b]
        + [full(a) for a in small],
        out_specs=[tok(D), tok(D), tok(n)],
        out_shape=[jax.ShapeDtypeStruct((B, S, D), F32),
                   jax.ShapeDtypeStruct((B, S, D), BF16),
                   jax.ShapeDtypeStruct((B, S, n), F32)],
        compiler_params=_cparams(2),
        name="merge",
    )(x, ya, bonus, g, yb, s1, s2, gt1.reshape(B, 1, D), sh2.reshape(B, 1, D),
      sc2.reshape(B, 1, D), *small)


def _moe_kernel(seg, x1_ref, h2_ref, meta_ref, metat_ref, w1_ref, w3_ref, w2_ref,
                gt2_ref, gf_ref, shf_ref, scf_ref, o_ref, buf_ref):
    b = pl.program_id(0)
    i = pl.program_id(1)
    g = pl.program_id(2)
    n_g = pl.num_programs(2)
    tm = h2_ref.shape[1]
    n_sub = tm // MOE_SUB
    W = MOE_WINDOW
    base = (b * pl.num_programs(1) + i) * (2 * n_sub * N_EXPERTS)

    def local(slots, s):
        return slots[:, s * MOE_SUB:(s + 1) * MOE_SUB] - float(s * MOE_CAP)

    pos1_row = metat_ref[0, 0:1, :]
    pos2_row = metat_ref[0, 1:2, :]
    wt1_row = metat_ref[0, 2:3, :]
    wt2_row = metat_ref[0, 3:4, :]

    @pl.when(g == 0)
    def _():
        blk = MXU_TILE
        for s in range(n_sub):
            p1 = local(pos1_row, s)
            p2 = local(pos2_row, s)
            h2s = h2_ref[0, s * MOE_SUB:(s + 1) * MOE_SUB, :]
            for r0 in range(0, MOE_CAP, blk):
                rid = lax.broadcasted_iota(jnp.int32, (blk, 1), 0).astype(F32) + float(r0)
                hot = jnp.where((p1 == rid) | (p2 == rid), 1.0, 0.0).astype(BF16)
                buf_ref[s * MOE_CAP + r0:s * MOE_CAP + r0 + blk, :] = jnp.dot(
                    hot, h2s, preferred_element_type=F32).astype(BF16)

    wrow = lax.broadcasted_iota(jnp.int32, (W, 1), 0)
    ge0 = g * EXPERTS_PER_GROUP
    first = [jnp.int32(0)]
    for e in range(EXPERTS_PER_GROUP):
        n_max = functools.reduce(
            jnp.maximum, [seg[base + (n_sub + s) * N_EXPERTS + ge0 + e] for s in range(n_sub)])
        first.append(first[-1] + (n_max + W - 1) // W)

    def items(it, carry):
        loaded = []
        for j in range(MOE_ITEMS):
            k = it * MOE_ITEMS + j
            e = functools.reduce(
                lambda a, c: a + (k >= c).astype(jnp.int32), first[1:EXPERTS_PER_GROUP], jnp.int32(0))
            w = k - functools.reduce(
                lambda a, ec: jnp.where(e == ec[0], ec[1], a), enumerate(first[:-1]), jnp.int32(0))
            starts = [seg[base + s * N_EXPERTS + ge0 + e] for s in range(n_sub)]
            counts = [seg[base + (n_sub + s) * N_EXPERTS + ge0 + e] for s in range(n_sub)]
            offs, xs = [], []
            for s in range(n_sub):
                off = jnp.minimum(starts[s] + w * W, (s + 1) * MOE_CAP - W)
                off = pl.multiple_of(off, MOE_ALIGN)
                offs.append(off)
                xs.append(buf_ref[pl.ds(off, W), :])
            loaded.append((e, w, starts, counts, offs, xs))
        results = []
        for e, w, starts, counts, offs, xs in loaded:
            xcat = jnp.concatenate(xs, axis=0)
            he = _silu(jnp.dot(xcat, w1_ref[e], preferred_element_type=F32)) * \
                jnp.dot(xcat, w3_ref[e], preferred_element_type=F32)
            results.append(_dot(he, w2_ref[e]))
        for (e, w, starts, counts, offs, xs), y in zip(loaded, results):
            for s in range(n_sub):
                rid = (starts[s] + w * W + wrow).astype(F32)
                cols = slice(s * MOE_SUB, (s + 1) * MOE_SUB)
                wt = jnp.sum(jnp.where(pos1_row[:, cols] == rid, wt1_row[:, cols], 0.0)
                             + jnp.where(pos2_row[:, cols] == rid, wt2_row[:, cols], 0.0),
                             axis=-1, keepdims=True)
                valid = wrow < counts[s] - w * W
                new = (wt * y[s * W:(s + 1) * W]).astype(BF16)
                buf_ref[pl.ds(offs[s], W), :] = jnp.where(valid, new, xs[s])
        return carry

    lax.fori_loop(0, (first[-1] + MOE_ITEMS - 1) // MOE_ITEMS, items, 0)

    @pl.when(g == n_g - 1)
    def _():
        meta = meta_ref[0]
        slot_ids = lax.broadcasted_iota(jnp.int32, (1, MOE_CAP), 1).astype(F32)
        for s in range(n_sub):
            rows = slice(s * MOE_SUB, (s + 1) * MOE_SUB)
            p1 = meta[rows, 0:1] - float(s * MOE_CAP)
            p2 = meta[rows, 1:2] - float(s * MOE_CAP)
            hot = jnp.where((p1 == slot_ids) | (p2 == slot_ids), 1.0, 0.0).astype(BF16)
            moe = jnp.dot(hot, buf_ref[s * MOE_CAP:(s + 1) * MOE_CAP, :],
                          preferred_element_type=F32)
            x2 = x1_ref[0, rows, :] + gt2_ref[0] * moe
            ms = jnp.mean(x2 * x2, axis=-1, keepdims=True)
            y = (x2 * lax.rsqrt(ms + RMS_EPS)) * gf_ref[...]
            o_ref[0, rows, :] = y * (1.0 + scf_ref[0]) + shf_ref[0]


def _moe(seg, x1, h2, meta, metat, w1, w3, w2, gt2, g_final, shf, scf, *, tm):
    B, S, D = x1.shape
    E, _, F = w1.shape
    G = EXPERTS_PER_GROUP
    assert E == N_EXPERTS and MOE_CAP % MXU_TILE == 0 and tm % MOE_SUB == 0
    tok = lambda n: pl.BlockSpec((1, tm, n), lambda b, s, g, c: (b, s, 0))
    per_b = pl.BlockSpec((1, 1, D), lambda b, s, g, c: (b, 0, 0))
    grid_spec = pltpu.PrefetchScalarGridSpec(
        num_scalar_prefetch=1,
        grid=(B, S // tm, E // G),
        in_specs=[tok(D), tok(D), tok(128),
                  pl.BlockSpec((1, 8, tm), lambda b, s, g, c: (b, 0, s)),
                  pl.BlockSpec((G, D, F), lambda b, s, g, c: (g, 0, 0)),
                  pl.BlockSpec((G, D, F), lambda b, s, g, c: (g, 0, 0)),
                  pl.BlockSpec((G, F, D), lambda b, s, g, c: (g, 0, 0)),
                  per_b, pl.BlockSpec((1, D), lambda b, s, g, c: (0, 0)), per_b, per_b],
        out_specs=tok(D),
        scratch_shapes=[pltpu.VMEM(((tm // MOE_SUB) * MOE_CAP, D), BF16)])
    return pl.pallas_call(
        _moe_kernel,
        grid_spec=grid_spec,
        out_shape=jax.ShapeDtypeStruct((B, S, D), F32),
        compiler_params=_cparams(3),
        name="moe",
    )(seg, x1, h2, meta, metat, w1, w3, w2, gt2.reshape(B, 1, D), g_final.reshape(1, D),
      shf.reshape(B, 1, D), scf.reshape(B, 1, D))


def _block_diag(w):
    G, n, _ = w.shape
    eye = jnp.eye(G, dtype=w.dtype)
    return (eye[:, None, :, None] * w[:, :, None, :]).reshape(G * n, G * n)


def kernel(x, c, w_ada, b_ada, g_mix, w_in, mu_rkv, mu_wag, w0, w1, w2, a0, a1, a2, g1, g2, k_k, k_a, r_k, lnx_g, lnx_b, conv_w, conv_b, w_rgate, b_rgate, w_igate, b_igate, lam, p_a, p_b, w_out, g_ffn, w_rg, b_rg, w_re, b_re, w1e, w3e, w2e, g_final, w_ada_f, b_ada_f):
    B, S, D = x.shape
    depth = w_ada.shape[0]
    assert depth == 1, "the final norm is fused into the last MoE call; one layer supported"
    A = w0.shape[-1]
    Bw = lam.shape[-1]

    c8 = jnp.zeros((8, D), F32).at[:B].set(c)
    modf = _ada(c8, w_ada_f, b_ada_f)[:B]
    shf, scf = modf[:, :D], modf[:, D:]

    head_ids = jnp.arange(QUAD * HEAD) // HEAD
    bd = (head_ids[:, None] == head_ids[None, :]).astype(BF16)

    for l in range(depth):
        mod = _ada(c8, w_ada[l], b_ada[l])[:B]
        sh1, sc1, gt1, sh2, sc2, gt2 = (mod[:, i * D:(i + 1) * D] for i in range(6))

        lowrank = jnp.concatenate([w1[l], a1[l], g1[l]], axis=1)
        mu_cols = jnp.concatenate([jnp.broadcast_to(mu_wag[l, 0][:, None], w1[l].shape),
                                   jnp.broadcast_to(mu_wag[l, 1][:, None], a1[l].shape),
                                   jnp.broadcast_to(mu_wag[l, 2][:, None], g1[l].shape)], axis=1)
        o1 = 3 * A
        wext = jnp.concatenate([w_in[l][:, :o1], mu_cols * lowrank, (1.0 - mu_cols) * lowrank,
                                w_in[l][:, o1:]], axis=1).astype(BF16)
        ra, rb = w2.shape[1], g2.shape[1]
        w2cat = jnp.zeros((2 * ra + rb, 3 * A), F32)
        w2cat = w2cat.at[0:ra, 0:A].set(w2[l]).at[ra:2 * ra, A:2 * A].set(a2[l])
        w2cat = w2cat.at[2 * ra:, 2 * A:].set(g2[l]).astype(BF16)

        per_tile = QUAD * HEAD // w_rgate.shape[-1]
        wide = lambda w: jax.vmap(_block_diag)(
            w.reshape(-1, per_tile, *w.shape[1:])).astype(BF16)
        (r, lw, k, v, kk, bvec, g, bonus, yb, s1, s2) = _inproj(
            x, sh1, sc1, g_mix[l], wext, mu_rkv[l], w0[l], a0[l], w2cat, k_k[l], k_a[l],
            r_k[l].reshape(-1), bd, conv_w[l].reshape(CONV_WIDTH, Bw), conv_b[l],
            wide(w_rgate[l]), b_rgate[l], wide(w_igate[l]), b_igate[l], lam[l])

        ya = _rwkv(r, lw, k, v, kk, bvec)

        n_g, n_e = w_rg.shape[-1], w_re.shape[-1]
        wr = jnp.zeros((D, 128), F32).at[:, :n_g].set(w_rg[l]).at[:, n_g:n_g + n_e].set(w_re[l])
        br = jnp.zeros((128,), F32).at[:n_g].set(b_rg[l]).at[n_g:n_g + n_e].set(b_re[l])
        x1, h2, logits = _merge(
            x, ya, bonus, g, yb, s1, s2, gt1, sh2, sc2, lnx_g[l], lnx_b[l], bd,
            p_a[l].astype(BF16), p_b[l].astype(BF16), w_out[l].astype(BF16),
            g_ffn[l], wr, br)
        meta, metat, seg = _router(logits, tm=MOE_TILE)
        seg = seg[:, :, :, n_g:n_g + n_e].astype(jnp.int32).reshape(-1)
        x = _moe(seg, x1, h2, meta, metat, w1e[l].astype(BF16), w3e[l].astype(BF16),
                 w2e[l].astype(BF16), gt2, g_final, shf, scf, tm=MOE_TILE)
    return x
```

```python
import functools

import jax
import jax.numpy as jnp
from jax import lax
from jax.experimental import pallas as pl
from jax.experimental.pallas import tpu as pltpu

F32 = jnp.float32
BF16 = jnp.bfloat16

MXU_TILE = 256
HEAD = 64
CHUNK = 64
SUB = 16
QUAD = 4
RWKV_STEP_CHUNKS = 4
RMS_EPS = 1e-6
GN_EPS = 64e-5
LRU_C = 8.0
CONV_WIDTH = 4
N_GROUPS = 4
EXPERTS_PER_GROUP = 8
N_EXPERTS = N_GROUPS * EXPERTS_PER_GROUP
MOE_TILE = 1024
MOE_SUB = 256
MOE_ALIGN = 16
MOE_WINDOW = 32
MOE_ITEMS = 4
MOE_CAP = 2 * MOE_SUB + N_EXPERTS * (MOE_ALIGN - 1) + MOE_WINDOW
VMEM_LIMIT = 58 * 1024 * 1024


def _cparams(n_axes):
    return pltpu.CompilerParams(dimension_semantics=("arbitrary",) * n_axes,
                                vmem_limit_bytes=VMEM_LIMIT)


def _dot(a, b, dims=(((1,), (0,)), ((), ()))):
    return lax.dot_general(a.astype(BF16), b.astype(BF16), dims,
                           preferred_element_type=F32)


NT = (((1,), (1,)), ((), ()))
TN = (((0,), (0,)), ((), ()))


def _split(a):
    hi = a.astype(BF16)
    lo = (a - hi.astype(F32)).astype(BF16)
    return hi, lo


def _dot_hl(a, b_exact, dims=(((1,), (0,)), ((), ()))):
    hi, lo = _split(a)
    return (lax.dot_general(hi, b_exact, dims, preferred_element_type=F32)
            + lax.dot_general(lo, b_exact, dims, preferred_element_type=F32))


def _head_sums(a, ones_blocks):
    w = ones_blocks.shape[0]
    parts = [_dot_hl(a[:, o:o + w], ones_blocks) for o in range(0, a.shape[1], w)]
    return parts[0] if len(parts) == 1 else jnp.concatenate(parts, axis=1)


def _dot3(a, b, dims=(((1,), (0,)), ((), ()))):
    ah, al = _split(a)
    bh, bl = _split(b)
    d = functools.partial(lax.dot_general, dimension_numbers=dims,
                          preferred_element_type=F32)
    return d(ah, bh) + (d(ah, bl) + d(al, bh))


def _sigmoid(z):
    return 0.5 * jnp.tanh(0.5 * z) + 0.5


def _softplus(z):
    return jnp.maximum(z, 0.0) + jnp.log1p(jnp.exp(-jnp.abs(z)))


def _silu(z):
    return z * _sigmoid(z)


def _gelu_tanh(z):
    return 0.5 * z * (1.0 + jnp.tanh(0.7978845608028654 * (z + 0.044715 * (z * z * z))))


def _ada_kernel(c_ref, w_ref, b_ref, o_ref):
    ca = _silu(c_ref[...])
    o_ref[...] = _dot3(ca, w_ref[...]) + b_ref[...]


def _ada(c8, w, b, tn=1024):
    d, n = w.shape
    return pl.pallas_call(
        _ada_kernel,
        grid=(n // tn,),
        in_specs=[pl.BlockSpec((8, d), lambda j: (0, 0)),
                  pl.BlockSpec((d, tn), lambda j: (0, j)),
                  pl.BlockSpec((1, tn), lambda j: (0, j))],
        out_specs=pl.BlockSpec((8, tn), lambda j: (0, j)),
        out_shape=jax.ShapeDtypeStruct((8, n), F32),
        compiler_params=_cparams(1),
        name="adaln",
    )(c8, w, b.reshape(1, n))


def _inproj_kernel(x_ref, sh_ref, sc_ref, g_ref, wext_ref, mu_ref, w0_ref, a0_ref,
                   w2cat_ref, kk_ref, ka_ref, rk_ref, bd_ref,
                   cw_ref, cb_ref, wr_ref, br_ref, wi_ref, bi_ref, lam_ref,
                   r_out, lw_out, k_out, v_out, kk_out, b_out, g_out, bonus_out,
                   yb_out, s1_out, s2_out, carry_ref, xc_carry, h_carry, *, a_width, b_width,
                   d_model):
    A = a_width
    n_shift = 3 * A + 256
    s = pl.program_id(1)

    @pl.when(s == 0)
    def _():
        carry_ref[...] = jnp.zeros_like(carry_ref)
        xc_carry[...] = jnp.zeros_like(xc_carry)
        h_carry[...] = jnp.zeros_like(h_carry)

    x = x_ref[0]
    tm = x.shape[0]
    ms = jnp.mean(x * x, axis=-1, keepdims=True)
    h = (x * lax.rsqrt(ms + RMS_EPS)) * g_ref[...]
    h = h * (1.0 + sc_ref[0]) + sh_ref[0]
    hb = h.astype(BF16)

    sg = jnp.dot(hb, wext_ref[:, 0:n_shift + 256], preferred_element_type=F32)
    o = n_shift + 256
    xg = jnp.dot(hb, wext_ref[:, o:o + 2 * b_width], preferred_element_type=F32)
    o += 2 * b_width

    def gate_strand():
        step = 2 * MXU_TILE
        for out, base in ((s1_out, o), (s2_out, o + d_model)):
            for c in range(0, d_model, step):
                out[0, :, c:c + step] = _sigmoid(jnp.dot(
                    hb, wext_ref[:, base + c:base + c + step], preferred_element_type=F32))
                yield

    def rwkv_strand():
        yield from _rwkv_heads(sg, n_shift, A, carry_ref, mu_ref, w0_ref, a0_ref, w2cat_ref,
                               kk_ref, ka_ref, rk_ref, bd_ref, r_out, lw_out, k_out, v_out,
                               kk_out, b_out, g_out, bonus_out)

    strands = [gate_strand(), rwkv_strand(),
               _rglru_tile(xg[:, 0:b_width], xg[:, b_width:2 * b_width], cw_ref, cb_ref, wr_ref,
                           br_ref, wi_ref, bi_ref, lam_ref, xc_carry, h_carry, yb_out)]
    while strands:
        strands = [c for c in strands if next(c, "done") != "done"]


def _rwkv_heads(sg, n_shift, A, carry_ref, mu_ref, w0_ref, a0_ref, w2cat_ref, kk_ref, ka_ref,
                rk_ref, bd_ref, r_out, lw_out, k_out, v_out, kk_out, b_out, g_out, bonus_out):
    tm = sg.shape[0]
    cur = sg[:, 0:n_shift]
    row = lax.broadcasted_iota(jnp.int32, (tm, 1), 0)
    prev = jnp.where(row == 0, carry_ref[0:1, :], pltpu.roll(cur, 1, axis=0))
    carry_ref[0:1, :] = cur[tm - 1:tm, :]

    rkv = cur[:, 0:3 * A]
    rkv = rkv + (prev[:, 0:3 * A] - rkv) * mu_ref[...]
    r = rkv[:, 0:A]
    k = rkv[:, A:2 * A]
    v = rkv[:, 2 * A:3 * A]
    r_out[0] = r
    v_out[0] = v
    yield

    pre = sg[:, n_shift:n_shift + 256] + prev[:, 3 * A:3 * A + 256]
    lane = lax.broadcasted_iota(jnp.int32, pre.shape, 1)
    act = jnp.where(lane < 64, jnp.tanh(pre), jnp.where(lane < 128, pre, _sigmoid(pre)))
    low = _dot(act, w2cat_ref[...])
    bd = bd_ref[...]
    kkr = k * kk_ref[...]
    kk_ss = _head_sums(kkr * kkr, bd)
    yield
    w_log = -_softplus(-(w0_ref[...] + low[:, 0:A])) - 0.5
    lw_out[0] = -jnp.exp(w_log)
    g_out[0] = low[:, 2 * A:3 * A]
    yield
    iclr = _sigmoid(a0_ref[...] + low[:, A:2 * A])
    kkn = kkr * lax.rsqrt(kk_ss + 1e-12)
    k2 = k * (1.0 + (iclr - 1.0) * ka_ref[...])
    k_out[0] = k2
    kk_out[0] = kkn
    b_out[0] = kkn * iclr
    yield
    bonus_out[0] = _head_sums(r * k2 * rk_ref[...], bd) * v


def _inproj(x, sh1, sc1, g_mix, wext, mu_rkv, w0, a0, w2cat, k_k, k_a, r_k, bd,
            conv_w, conv_b, wr, br, wi, bi, lam, *, tm=256):
    B, S, D = x.shape
    A = w0.shape[-1]
    Bw = lam.shape[-1]
    row = lambda a: a.reshape(1, -1)
    full = lambda a: pl.BlockSpec(a.shape, lambda b, s: (0,) * a.ndim)
    tok = lambda n: pl.BlockSpec((1, tm, n), lambda b, s: (b, s, 0))
    per_b = pl.BlockSpec((1, 1, D), lambda b, s: (b, 0, 0))
    small = [row(g_mix), wext, row(mu_rkv), row(w0), row(a0), w2cat, row(k_k), row(k_a),
             row(r_k), bd, conv_w, row(conv_b), wr, row(br), wi, row(bi), row(lam)]
    outs = [jax.ShapeDtypeStruct((B, S, A), F32)] * 8 + \
           [jax.ShapeDtypeStruct((B, S, Bw), F32)] + \
           [jax.ShapeDtypeStruct((B, S, D), F32)] * 2
    return pl.pallas_call(
        functools.partial(_inproj_kernel, a_width=A, b_width=Bw, d_model=D),
        grid=(B, S // tm),
        in_specs=[tok(D), per_b, per_b] + [full(a) for a in small],
        out_specs=[tok(A)] * 8 + [tok(Bw)] + [tok(D)] * 2,
        out_shape=outs,
        scratch_shapes=[pltpu.VMEM((8, 3 * A + 256), F32), pltpu.VMEM((8, Bw), F32),
                        pltpu.VMEM((8, Bw), F32)],
        compiler_params=_cparams(2),
        name="inproj",
    )(x, sh1.reshape(B, 1, D), sc1.reshape(B, 1, D), *small)


def _rwkv_kernel(r_ref, lw_ref, k_ref, v_ref, kk_ref, b_ref, tri_ref, bdm_ref, slm_ref,
                 lm_ref, dgm_ref, y_ref, s_ref):
    @pl.when(pl.program_id(0) == 0)
    def _():
        s_ref[...] = jnp.zeros_like(s_ref)

    W = QUAD * HEAD
    n_b, rows, n_a = r_ref.shape
    masks = (tri_ref[...], bdm_ref[...], slm_ref[...], lm_ref[...], dgm_ref[...])
    chains = []
    state_version = {}
    for c in range(rows // CHUNK):
        t = slice(c * CHUNK, (c + 1) * CHUNK)
        for bi in range(n_b):
            for q in range(n_a // W):
                cols = slice(q * W, (q + 1) * W)
                idx = bi * (n_a // W) + q
                ins = [ref[bi, t, cols] for ref in (r_ref, lw_ref, k_ref, v_ref, kk_ref, b_ref)]
                chains.append(_rwkv_chunk(*ins, s_ref, idx, masks, state_version, c,
                                          functools.partial(_store_y, y_ref, bi, t, cols)))
    while chains:
        chains = [c for c in chains if next(c, "done") != "done"]


def _store_y(y_ref, bi, t, cols, y):
    y_ref[bi, t, cols] = y


def _rwkv_chunk(r, lw, k, v, kk, b, s_ref, idx, masks, state_version, version, store_y):
    tri, bdm, slm, lm, dgm = masks
    L = CHUNK
    a = -kk

    cl = _dot_hl3(tri, lw)
    clp = cl - lw
    cm = cl[L // 2 - 1:L // 2, :]
    ce = cl[L - 1:L, :]
    at = a * jnp.exp(clp - cm)
    rt = r * jnp.exp(cl - cm)
    e_inv = jnp.exp(cm - cl)
    bt = b * e_inv
    kt = k * e_inv
    e_end = jnp.exp(ce - cl)
    bh = b * e_end
    kh = k * e_end
    g_mid = jnp.exp(cm)
    g_end = jnp.exp(ce)

    tile = lambda m: jnp.concatenate([m] * QUAD, axis=0)
    stack = lambda m: tile(m) * bdm
    a_s = stack(at)
    r_s = stack(rt)
    v_s = stack(v)
    bh_s = stack(bh)
    kh_s = stack(kh)
    bt_t = tile(bt)
    kt_t = tile(kt)
    yield

    a_ab = _dot(a_s, bt_t, NT) * slm
    a_ak = _dot(a_s, kt_t, NT) * slm
    a_rb = _dot(r_s, bt_t, NT) * lm
    a_rk = _dot(r_s, kt_t, NT) * lm
    yield

    d = a_ab * dgm
    e = a_ab - d
    p = d
    d_inv = (lm - slm) + p
    n = 2
    while n < SUB:
        p = _dot(p, p)
        yield
        d_inv = d_inv + _dot(d_inv, p)
        yield
        n *= 2
    f = _dot(d_inv, e)
    yield
    t_inv = d_inv
    fp = f
    n = 1
    terms = []
    while n < L // SUB:
        terms.append(fp)
        n *= 2
        if n < L // SUB:
            fp = _dot(fp, fp)
            yield
    for fp in reversed(terms):
        t_inv = t_inv + _dot(fp, t_inv)
        yield

    while state_version.get(idx, 0) != version:
        yield
    s0 = s_ref[idx]
    s_mid = s0 * g_mid
    rhs = _dot(a_s, s_mid, NT) + _dot(a_ak, v_s)
    y_part = _dot(r_s, s_mid, NT) + _dot(a_rk, v_s)
    yield
    u = _dot(t_inv, rhs)
    yield
    y_s = y_part + _dot(a_rb, u)
    s_ref[idx] = s0 * g_end + _dot(u, bh_s, TN) + _dot(v_s, kh_s, TN)
    state_version[idx] = version + 1
    y = y_s[0:L]
    for q in range(1, QUAD):
        y = y + y_s[q * L:(q + 1) * L]
    store_y(y)


def _dot_hl3(tri, lw):
    hi = lw.astype(BF16)
    r1 = lw - hi.astype(F32)
    mid = r1.astype(BF16)
    lo = (r1 - mid.astype(F32)).astype(BF16)
    d = functools.partial(jnp.dot, preferred_element_type=F32)
    return d(tri, hi) + (d(tri, mid) + d(tri, lo))


def _rwkv(r, lw, k, v, kk, b):
    B, S, A = r.shape
    W = QUAD * HEAD
    L = CHUNK
    idx = jnp.arange(W)
    assert L == HEAD, "stacked rows (head, time) and lanes (head, channel) share one block size"
    same = (idx[:, None] // L) == (idx[None, :] // L)
    ti = idx[:, None] % L
    tj = idx[None, :] % L
    bdm = same.astype(F32)
    slm = (same & (ti > tj)).astype(F32)
    lm = (same & (ti >= tj)).astype(F32)
    dgm = (same & (ti > tj) & (ti // SUB == tj // SUB)).astype(F32)
    tri = (jnp.arange(L)[:, None] >= jnp.arange(L)[None, :]).astype(BF16)
    rows = RWKV_STEP_CHUNKS * L
    tok = pl.BlockSpec((B, rows, A), lambda c: (0, c, 0))
    full = lambda a: pl.BlockSpec(a.shape, lambda c: (0,) * a.ndim)
    return pl.pallas_call(
        _rwkv_kernel,
        grid=(S // rows,),
        in_specs=[tok] * 6 + [full(tri), full(bdm), full(slm), full(lm), full(dgm)],
        out_specs=tok,
        out_shape=jax.ShapeDtypeStruct((B, S, A), F32),
        scratch_shapes=[pltpu.VMEM((B * (A // W), W, W), F32)],
        compiler_params=_cparams(1),
        name="rwkv7",
    )(r, lw, k, v, kk, b, tri, bdm, slm, lm, dgm)


def _wide_block_dot(a, w_ref):
    w = w_ref.shape[-1]
    parts = [_dot(a[:, i * w:(i + 1) * w], w_ref[i]) for i in range(w_ref.shape[0])]
    return parts[0] if len(parts) == 1 else jnp.concatenate(parts, axis=1)


def _rglru_tile(xb, gb, cw_ref, cb_ref, wr_ref, br_ref, wi_ref, bi_ref, lam_ref,
                xc_carry, h_carry, y_out):
    tt, width = xb.shape
    ext = jnp.concatenate([xc_carry[...], xb], axis=0)
    xc_carry[...] = xb[tt - 8:tt, :]
    xc = cb_ref[...] + cw_ref[CONV_WIDTH - 1:CONV_WIDTH, :] * xb
    for d in range(1, CONV_WIDTH):
        xc = xc + cw_ref[CONV_WIDTH - 1 - d:CONV_WIDTH - d, :] * ext[8 - d:8 - d + tt, :]

    pre_r = _wide_block_dot(xc, wr_ref)
    pre_i = _wide_block_dot(xc, wi_ref)
    yield
    gate_r = _sigmoid(pre_r + br_ref[...])
    gate_i = _sigmoid(pre_i + bi_ref[...])
    log_a = -LRU_C * gate_r * _softplus(-lam_ref[...])
    a = jnp.exp(log_a)
    u = jnp.sqrt(-jnp.tanh(log_a) * (a * a + 1.0)) * (gate_i * xc)
    yield

    a3 = a.reshape(tt // 8, 8, width)
    u3 = u.reshape(tt // 8, 8, width)
    sub = lax.broadcasted_iota(jnp.int32, (1, 8, 1), 1)
    d = 1
    while d < 8:
        keep = sub >= d
        a_sh = jnp.where(keep, pltpu.roll(a3, d, axis=1), 1.0)
        u_sh = jnp.where(keep, pltpu.roll(u3, d, axis=1), 0.0)
        u3 = a3 * u_sh + u3
        a3 = a3 * a_sh
        d *= 2
        yield
    gate = _gelu_tanh(gb)
    carry = h_carry[0:1, :]
    groups = []
    for i in range(tt // 8):
        hg = a3[i] * carry + u3[i]
        groups.append(hg)
        carry = hg[7:8, :]
        if i % 8 == 7:
            yield
    h_carry[0:1, :] = carry
    y_out[0] = jnp.concatenate(groups, axis=0) * gate


def _merge_kernel(x_ref, ya_ref, bonus_ref, g_ref, yb_ref, s1_ref, s2_ref, gt1_ref, sh2_ref,
                  sc2_ref, lng_ref, lnb_ref, bd_ref, pa_ref, pb_ref, wout_ref, gffn_ref,
                  wr_ref, br_ref, x1_out, h2_out, logit_out):
    ya = ya_ref[0]
    bd = bd_ref[...]
    inv_n = 1.0 / HEAD
    mean = _head_sums(ya, bd) * inv_n
    yc = ya - mean
    var = _head_sums(yc * yc, bd) * inv_n
    yn = yc * lax.rsqrt(var + GN_EPS) * lng_ref[...] + lnb_ref[...]
    ya2 = (yn + bonus_ref[0]) * g_ref[0]

    merged = s1_ref[0] * _dot(ya2, pa_ref[...]) + s2_ref[0] * _dot(yb_ref[0], pb_ref[...])
    x1 = x_ref[0] + gt1_ref[0] * _dot(merged, wout_ref[...])
    x1_out[0] = x1

    ms = jnp.mean(x1 * x1, axis=-1, keepdims=True)
    h2 = (x1 * lax.rsqrt(ms + RMS_EPS)) * gffn_ref[...]
    h2 = h2 * (1.0 + sc2_ref[0]) + sh2_ref[0]
    h2_out[0] = h2.astype(BF16)

    logit_out[0] = _dot3(h2, wr_ref[...]) + br_ref[...]


def _router_kernel(logit_ref, tri_ref, utri_ref, meta_out, metat_out, seg_out):
    logits = logit_ref[0]
    lane = lax.broadcasted_iota(jnp.int32, logits.shape, 1).astype(F32)
    neg = jnp.float32(-jnp.inf)
    big = jnp.float32(1e9)
    is_g = lane < N_GROUPS
    lg = jnp.where(is_g, logits, neg)
    mg = jnp.max(lg, axis=-1, keepdims=True)
    pg_top = 1.0 / jnp.sum(jnp.exp(lg - mg), axis=-1, keepdims=True)
    g_idx = jnp.min(jnp.where(lg == mg, lane, big), axis=-1, keepdims=True)
    e_lane = lane - N_GROUPS
    in_grp = (e_lane >= g_idx * EXPERTS_PER_GROUP) & (e_lane < (g_idx + 1) * EXPERTS_PER_GROUP)
    le = jnp.where(in_grp, logits, neg)
    me = jnp.max(le, axis=-1, keepdims=True)
    i1 = jnp.min(jnp.where(le == me, lane, big), axis=-1, keepdims=True)
    ee = jnp.exp(le - me)
    se = jnp.sum(ee, axis=-1, keepdims=True)
    rest = jnp.where(lane != i1, le, neg)
    m2 = jnp.max(rest, axis=-1, keepdims=True)
    i2 = jnp.min(jnp.where(rest == m2, lane, big), axis=-1, keepdims=True)
    p1 = 1.0 / se
    p2 = jnp.exp(m2 - me) / se
    den = p1 + p2
    wt1 = pg_top * (p1 / den)
    wt2 = pg_top * (p2 / den)

    hot1 = jnp.where(lane == i1, 1.0, 0.0)
    hot2 = jnp.where(lane == i2, 1.0, 0.0)
    both = hot1 + hot2
    n_sub = logits.shape[0] // MOE_SUB
    pos1, pos2, seg_rows = [], [], []
    for s in range(n_sub):
        rows = slice(s * MOE_SUB, (s + 1) * MOE_SUB)
        before = _dot(tri_ref[...], both[rows])
        count = jnp.sum(both[rows], axis=0, keepdims=True)
        padded = jnp.floor((count + (MOE_ALIGN - 1)) * (1.0 / MOE_ALIGN)) * MOE_ALIGN
        start = _dot_hl(jnp.broadcast_to(padded, (8, padded.shape[1])), utri_ref[...])[0:1] \
            + float(s * MOE_CAP)
        slot = before + start
        pos1.append(jnp.sum(hot1[rows] * slot, axis=-1, keepdims=True))
        pos2.append(jnp.sum(hot2[rows] * slot, axis=-1, keepdims=True))
        seg_rows.append((start, count))
    pos1 = jnp.concatenate(pos1, axis=0)
    pos2 = jnp.concatenate(pos2, axis=0)
    meta = jnp.where(lane == 0, pos1, jnp.where(lane == 1, pos2,
                     jnp.where(lane == 2, wt1, jnp.where(lane == 3, wt2, 0.0))))
    meta_out[0] = meta
    metat_out[0] = jnp.transpose(meta)[0:8, :]
    seg = [st for st, _ in seg_rows] + [ct for _, ct in seg_rows]
    seg_out[0, 0] = jnp.concatenate(seg, axis=0)


def _router(logits, *, tm):
    B, S, n = logits.shape
    assert 2 * (tm // MOE_SUB) == 8, "segment table holds one start and one count row per sub-tile"
    tri = (jnp.arange(MOE_SUB)[:, None] > jnp.arange(MOE_SUB)[None, :]).astype(BF16)
    utri = (jnp.arange(n)[:, None] < jnp.arange(n)[None, :]).astype(BF16)
    tok = pl.BlockSpec((1, tm, n), lambda b, s: (b, s, 0))
    full = lambda a: pl.BlockSpec(a.shape, lambda b, s: (0,) * a.ndim)
    return pl.pallas_call(
        _router_kernel,
        grid=(B, S // tm),
        in_specs=[tok, full(tri), full(utri)],
        out_specs=[tok, pl.BlockSpec((1, 8, tm), lambda b, s: (b, 0, s)),
                   pl.BlockSpec((1, 1, 8, n), lambda b, s: (b, s, 0, 0))],
        out_shape=[jax.ShapeDtypeStruct((B, S, n), F32),
                   jax.ShapeDtypeStruct((B, 8, S), F32),
                   jax.ShapeDtypeStruct((B, S // tm, 8, n), F32)],
        compiler_params=_cparams(2),
        name="router",
    )(logits, tri, utri)


def _merge(x, ya, bonus, g, yb, s1, s2, gt1, sh2, sc2, lnx_g, lnx_b, bd, pa, pb, wout, g_ffn,
           wr, br, *, tm=512):
    B, S, D = x.shape
    A = ya.shape[-1]
    n = wr.shape[1]
    row = lambda a: a.reshape(1, -1)
    full = lambda a: pl.BlockSpec(a.shape, lambda b, s: (0,) * a.ndim)
    tok = lambda n: pl.BlockSpec((1, tm, n), lambda b, s: (b, s, 0))
    per_b = pl.BlockSpec((1, 1, D), lambda b, s: (b, 0, 0))
    small = [row(lnx_g), row(lnx_b), bd, pa, pb, wout, row(g_ffn), wr, row(br)]
    return pl.pallas_call(
        _merge_kernel,
        grid=(B, S // tm),
        in_specs=[tok(D), tok(A), tok(A), tok(A), tok(A), tok(D), tok(D), per_b, per_b, per_b]
        + [full(a) for a in small],
        out_specs=[tok(D), tok(D), tok(n)],
        out_shape=[jax.ShapeDtypeStruct((B, S, D), F32),
                   jax.ShapeDtypeStruct((B, S, D), BF16),
                   jax.ShapeDtypeStruct((B, S, n), F32)],
        compiler_params=_cparams(2),
        name="merge",
    )(x, ya, bonus, g, yb, s1, s2, gt1.reshape(B, 1, D), sh2.reshape(B, 1, D),
      sc2.reshape(B, 1, D), *small)


def _moe_kernel(seg, x1_ref, h2_ref, meta_ref, metat_ref, w1_ref, w3_ref, w2_ref,
                gt2_ref, gf_ref, shf_ref, scf_ref, o_ref, buf_ref, hid_ref):
    b = pl.program_id(0)
    i = pl.program_id(1)
    g = pl.program_id(2)
    n_g = pl.num_programs(2)
    tm = h2_ref.shape[1]
    n_sub = tm // MOE_SUB
    W = MOE_WINDOW
    base = (b * pl.num_programs(1) + i) * (2 * n_sub * N_EXPERTS)

    def local(slots, s):
        return slots[:, s * MOE_SUB:(s + 1) * MOE_SUB] - float(s * MOE_CAP)

    pos1_row = metat_ref[0, 0:1, :]
    pos2_row = metat_ref[0, 1:2, :]
    wt1_row = metat_ref[0, 2:3, :]
    wt2_row = metat_ref[0, 3:4, :]

    @pl.when(g == 0)
    def _():
        blk = MXU_TILE
        for s in range(n_sub):
            p1 = local(pos1_row, s)
            p2 = local(pos2_row, s)
            h2s = h2_ref[0, s * MOE_SUB:(s + 1) * MOE_SUB, :]
            for r0 in range(0, MOE_CAP, blk):
                rid = lax.broadcasted_iota(jnp.int32, (blk, 1), 0).astype(F32) + float(r0)
                hot = jnp.where((p1 == rid) | (p2 == rid), 1.0, 0.0).astype(BF16)
                buf_ref[s * MOE_CAP + r0:s * MOE_CAP + r0 + blk, :] = jnp.dot(
                    hot, h2s, preferred_element_type=F32).astype(BF16)

    wrow = lax.broadcasted_iota(jnp.int32, (W, 1), 0)
    ge0 = g * EXPERTS_PER_GROUP
    first = [jnp.int32(0)]
    for e in range(EXPERTS_PER_GROUP):
        n_max = functools.reduce(
            jnp.maximum, [seg[base + (n_sub + s) * N_EXPERTS + ge0 + e] for s in range(n_sub)])
        first.append(first[-1] + (n_max + W - 1) // W)
    total = first[-1]
    n_f = w1_ref.shape[-1]

    def locate(k):
        inside = jnp.logical_and(k >= 0, k < total)
        e = functools.reduce(lambda a, c: a + (k >= c).astype(jnp.int32),
                             first[1:EXPERTS_PER_GROUP], jnp.int32(0))
        w = k - functools.reduce(lambda a, ec: jnp.where(e == ec[0], ec[1], a),
                                 enumerate(first[:-1]), jnp.int32(0))
        e = jnp.where(inside, e, EXPERTS_PER_GROUP - 1)
        w = jnp.where(inside, w, MOE_CAP // W)
        starts = [seg[base + s * N_EXPERTS + ge0 + e] for s in range(n_sub)]
        counts = [seg[base + (n_sub + s) * N_EXPERTS + ge0 + e] for s in range(n_sub)]
        offs = [pl.multiple_of(jnp.minimum(starts[s] + w * W, (s + 1) * MOE_CAP - W), MOE_ALIGN)
                for s in range(n_sub)]
        return e, w, offs, starts, counts

    def read_windows(it):
        out = []
        for j in range(MOE_ITEMS):
            e, w, offs, starts, counts = locate(it * MOE_ITEMS + j)
            out.append((e, jnp.concatenate(
                [buf_ref[pl.ds(offs[s], W), :] for s in range(n_sub)], axis=0)))
        return out

    def project_up(windows):
        for j, (e, xcat) in enumerate(windows):
            hid_ref[j, :, 0:n_f] = jnp.dot(xcat, w1_ref[e], preferred_element_type=F32)
            hid_ref[j, :, n_f:2 * n_f] = jnp.dot(xcat, w3_ref[e], preferred_element_type=F32)

    def finish(it, hidden):
        for j in range(MOE_ITEMS):
            e, w, offs, starts, counts = locate(it * MOE_ITEMS + j)
            hid = hidden[j]
            he = _silu(hid[:, 0:n_f]) * hid[:, n_f:2 * n_f]
            y = _dot(he, w2_ref[e])
            for s in range(n_sub):
                rid = (starts[s] + w * W + wrow).astype(F32)
                cols = slice(s * MOE_SUB, (s + 1) * MOE_SUB)
                wt = jnp.sum(jnp.where(pos1_row[:, cols] == rid, wt1_row[:, cols], 0.0)
                             + jnp.where(pos2_row[:, cols] == rid, wt2_row[:, cols], 0.0),
                             axis=-1, keepdims=True)
                valid = wrow < counts[s] - w * W
                new = (wt * y[s * W:(s + 1) * W]).astype(BF16)
                old = buf_ref[pl.ds(offs[s], W), :]
                buf_ref[pl.ds(offs[s], W), :] = jnp.where(valid, new, old)

    def pipeline(it, carry):
        windows = read_windows(it)
        hidden = [hid_ref[j] for j in range(MOE_ITEMS)]
        project_up(windows)
        finish(it - 1, hidden)
        return carry

    n_iter = (total + MOE_ITEMS - 1) // MOE_ITEMS

    @pl.when(total > 0)
    def _():
        project_up(read_windows(0))

    lax.fori_loop(1, n_iter, pipeline, 0)

    @pl.when(total > 0)
    def _():
        finish(n_iter - 1, [hid_ref[j] for j in range(MOE_ITEMS)])

    @pl.when(g == n_g - 1)
    def _():
        meta = meta_ref[0]
        slot_ids = lax.broadcasted_iota(jnp.int32, (1, MOE_CAP), 1).astype(F32)
        for s in range(n_sub):
            rows = slice(s * MOE_SUB, (s + 1) * MOE_SUB)
            p1 = meta[rows, 0:1] - float(s * MOE_CAP)
            p2 = meta[rows, 1:2] - float(s * MOE_CAP)
            hot = jnp.where((p1 == slot_ids) | (p2 == slot_ids), 1.0, 0.0).astype(BF16)
            moe = jnp.dot(hot, buf_ref[s * MOE_CAP:(s + 1) * MOE_CAP, :],
                          preferred_element_type=F32)
            x2 = x1_ref[0, rows, :] + gt2_ref[0] * moe
            ms = jnp.mean(x2 * x2, axis=-1, keepdims=True)
            y = (x2 * lax.rsqrt(ms + RMS_EPS)) * gf_ref[...]
            o_ref[0, rows, :] = y * (1.0 + scf_ref[0]) + shf_ref[0]


def _moe(seg, x1, h2, meta, metat, w1, w3, w2, gt2, g_final, shf, scf, *, tm):
    B, S, D = x1.shape
    E, _, F = w1.shape
    G = EXPERTS_PER_GROUP
    assert E == N_EXPERTS and MOE_CAP % MXU_TILE == 0 and tm % MOE_SUB == 0
    tok = lambda n: pl.BlockSpec((1, tm, n), lambda b, s, g, c: (b, s, 0))
    per_b = pl.BlockSpec((1, 1, D), lambda b, s, g, c: (b, 0, 0))
    grid_spec = pltpu.PrefetchScalarGridSpec(
        num_scalar_prefetch=1,
        grid=(B, S // tm, E // G),
        in_specs=[tok(D), tok(D), tok(128),
                  pl.BlockSpec((1, 8, tm), lambda b, s, g, c: (b, 0, s)),
                  pl.BlockSpec((G, D, F), lambda b, s, g, c: (g, 0, 0)),
                  pl.BlockSpec((G, D, F), lambda b, s, g, c: (g, 0, 0)),
                  pl.BlockSpec((G, F, D), lambda b, s, g, c: (g, 0, 0)),
                  per_b, pl.BlockSpec((1, D), lambda b, s, g, c: (0, 0)), per_b, per_b],
        out_specs=tok(D),
        scratch_shapes=[pltpu.VMEM(((tm // MOE_SUB) * MOE_CAP, D), BF16),
                        pltpu.VMEM((MOE_ITEMS, (tm // MOE_SUB) * MOE_WINDOW, 2 * F), F32)])
    return pl.pallas_call(
        _moe_kernel,
        grid_spec=grid_spec,
        out_shape=jax.ShapeDtypeStruct((B, S, D), F32),
        compiler_params=_cparams(3),
        name="moe",
    )(seg, x1, h2, meta, metat, w1, w3, w2, gt2.reshape(B, 1, D), g_final.reshape(1, D),
      shf.reshape(B, 1, D), scf.reshape(B, 1, D))


def _block_diag(w):
    G, n, _ = w.shape
    eye = jnp.eye(G, dtype=w.dtype)
    return (eye[:, None, :, None] * w[:, :, None, :]).reshape(G * n, G * n)


def kernel(x, c, w_ada, b_ada, g_mix, w_in, mu_rkv, mu_wag, w0, w1, w2, a0, a1, a2, g1, g2, k_k, k_a, r_k, lnx_g, lnx_b, conv_w, conv_b, w_rgate, b_rgate, w_igate, b_igate, lam, p_a, p_b, w_out, g_ffn, w_rg, b_rg, w_re, b_re, w1e, w3e, w2e, g_final, w_ada_f, b_ada_f):
    B, S, D = x.shape
    depth = w_ada.shape[0]
    assert depth == 1, "the final norm is fused into the last MoE call; one layer supported"
    A = w0.shape[-1]
    Bw = lam.shape[-1]

    c8 = jnp.zeros((8, D), F32).at[:B].set(c)
    modf = _ada(c8, w_ada_f, b_ada_f)[:B]
    shf, scf = modf[:, :D], modf[:, D:]

    head_ids = jnp.arange(QUAD * HEAD) // HEAD
    bd = (head_ids[:, None] == head_ids[None, :]).astype(BF16)

    for l in range(depth):
        mod = _ada(c8, w_ada[l], b_ada[l])[:B]
        sh1, sc1, gt1, sh2, sc2, gt2 = (mod[:, i * D:(i + 1) * D] for i in range(6))

        lowrank = jnp.concatenate([w1[l], a1[l], g1[l]], axis=1)
        mu_cols = jnp.concatenate([jnp.broadcast_to(mu_wag[l, 0][:, None], w1[l].shape),
                                   jnp.broadcast_to(mu_wag[l, 1][:, None], a1[l].shape),
                                   jnp.broadcast_to(mu_wag[l, 2][:, None], g1[l].shape)], axis=1)
        o1 = 3 * A
        wext = jnp.concatenate([w_in[l][:, :o1], mu_cols * lowrank, (1.0 - mu_cols) * lowrank,
                                w_in[l][:, o1:]], axis=1).astype(BF16)
        ra, rb = w2.shape[1], g2.shape[1]
        w2cat = jnp.zeros((2 * ra + rb, 3 * A), F32)
        w2cat = w2cat.at[0:ra, 0:A].set(w2[l]).at[ra:2 * ra, A:2 * A].set(a2[l])
        w2cat = w2cat.at[2 * ra:, 2 * A:].set(g2[l]).astype(BF16)

        per_tile = QUAD * HEAD // w_rgate.shape[-1]
        wide = lambda w: jax.vmap(_block_diag)(
            w.reshape(-1, per_tile, *w.shape[1:])).astype(BF16)
        (r, lw, k, v, kk, bvec, g, bonus, yb, s1, s2) = _inproj(
            x, sh1, sc1, g_mix[l], wext, mu_rkv[l], w0[l], a0[l], w2cat, k_k[l], k_a[l],
            r_k[l].reshape(-1), bd, conv_w[l].reshape(CONV_WIDTH, Bw), conv_b[l],
            wide(w_rgate[l]), b_rgate[l], wide(w_igate[l]), b_igate[l], lam[l])

        ya = _rwkv(r, lw, k, v, kk, bvec)

        n_g, n_e = w_rg.shape[-1], w_re.shape[-1]
        wr = jnp.zeros((D, 128), F32).at[:, :n_g].set(w_rg[l]).at[:, n_g:n_g + n_e].set(w_re[l])
        br = jnp.zeros((128,), F32).at[:n_g].set(b_rg[l]).at[n_g:n_g + n_e].set(b_re[l])
        x1, h2, logits = _merge(
            x, ya, bonus, g, yb, s1, s2, gt1, sh2, sc2, lnx_g[l], lnx_b[l], bd,
            p_a[l].astype(BF16), p_b[l].astype(BF16), w_out[l].astype(BF16),
            g_ffn[l], wr, br)
        meta, metat, seg = _router(logits, tm=MOE_TILE)
        seg = seg[:, :, :, n_g:n_g + n_e].astype(jnp.int32).reshape(-1)
        x = _moe(seg, x1, h2, meta, metat, w1e[l].astype(BF16), w3e[l].astype(BF16),
                 w2e[l].astype(BF16), gt2, g_final, shf, scf, tm=MOE_TILE)
    return x
```

```python
import functools

import jax
import jax.numpy as jnp
from jax import lax
from jax.experimental import pallas as pl
from jax.experimental.pallas import tpu as pltpu

F32 = jnp.float32
BF16 = jnp.bfloat16

MXU_TILE = 256
HEAD = 64
CHUNK = 64
SUB = 16
QUAD = 4
RWKV_STEP_CHUNKS = 4
RMS_EPS = 1e-6
GN_EPS = 64e-5
LRU_C = 8.0
CONV_WIDTH = 4
N_GROUPS = 4
EXPERTS_PER_GROUP = 8
N_EXPERTS = N_GROUPS * EXPERTS_PER_GROUP
MOE_TILE = 1024
MOE_SUB = 256
MOE_ALIGN = 16
MOE_WINDOW = 32
MOE_ITEMS = 4
MOE_CAP = 2 * MOE_SUB + N_EXPERTS * (MOE_ALIGN - 1) + MOE_WINDOW
VMEM_LIMIT = 58 * 1024 * 1024


def _cparams(n_axes):
    return pltpu.CompilerParams(dimension_semantics=("arbitrary",) * n_axes,
                                vmem_limit_bytes=VMEM_LIMIT)


def _dot(a, b, dims=(((1,), (0,)), ((), ()))):
    return lax.dot_general(a.astype(BF16), b.astype(BF16), dims,
                           preferred_element_type=F32)


NT = (((1,), (1,)), ((), ()))
TN = (((0,), (0,)), ((), ()))


def _split(a):
    hi = a.astype(BF16)
    lo = (a - hi.astype(F32)).astype(BF16)
    return hi, lo


def _dot_hl(a, b_exact, dims=(((1,), (0,)), ((), ()))):
    hi, lo = _split(a)
    return (lax.dot_general(hi, b_exact, dims, preferred_element_type=F32)
            + lax.dot_general(lo, b_exact, dims, preferred_element_type=F32))


def _head_sums(a, ones_blocks):
    w = ones_blocks.shape[0]
    parts = [_dot_hl(a[:, o:o + w], ones_blocks) for o in range(0, a.shape[1], w)]
    return parts[0] if len(parts) == 1 else jnp.concatenate(parts, axis=1)


def _dot3(a, b, dims=(((1,), (0,)), ((), ()))):
    ah, al = _split(a)
    bh, bl = _split(b)
    d = functools.partial(lax.dot_general, dimension_numbers=dims,
                          preferred_element_type=F32)
    return d(ah, bh) + (d(ah, bl) + d(al, bh))


def _sigmoid(z):
    return 0.5 * jnp.tanh(0.5 * z) + 0.5


def _softplus(z):
    return jnp.maximum(z, 0.0) + jnp.log1p(jnp.exp(-jnp.abs(z)))


def _silu(z):
    return z * _sigmoid(z)


def _gelu_tanh(z):
    return 0.5 * z * (1.0 + jnp.tanh(0.7978845608028654 * (z + 0.044715 * (z * z * z))))


def _ada_kernel(c_ref, w_ref, b_ref, o_ref):
    ca = _silu(c_ref[...])
    o_ref[...] = _dot3(ca, w_ref[...]) + b_ref[...]


def _ada(c8, w, b, tn=1024):
    d, n = w.shape
    return pl.pallas_call(
        _ada_kernel,
        grid=(n // tn,),
        in_specs=[pl.BlockSpec((8, d), lambda j: (0, 0)),
                  pl.BlockSpec((d, tn), lambda j: (0, j)),
                  pl.BlockSpec((1, tn), lambda j: (0, j))],
        out_specs=pl.BlockSpec((8, tn), lambda j: (0, j)),
        out_shape=jax.ShapeDtypeStruct((8, n), F32),
        compiler_params=_cparams(1),
        name="adaln",
    )(c8, w, b.reshape(1, n))


def _inproj_kernel(x_ref, sh_ref, sc_ref, g_ref, wext_ref, mu_ref, w0_ref, a0_ref,
                   w2cat_ref, kk_ref, ka_ref, rk_ref, bd_ref,
                   cw_ref, cb_ref, wr_ref, br_ref, wi_ref, bi_ref, lam_ref,
                   r_out, lw_out, k_out, v_out, kk_out, b_out, g_out, bonus_out,
                   yb_out, s1_out, s2_out, carry_ref, xc_carry, h_carry, *, a_width, b_width,
                   d_model):
    A = a_width
    n_shift = 3 * A + 256
    s = pl.program_id(1)

    @pl.when(s == 0)
    def _():
        carry_ref[...] = jnp.zeros_like(carry_ref)
        xc_carry[...] = jnp.zeros_like(xc_carry)
        h_carry[...] = jnp.zeros_like(h_carry)

    x = x_ref[0]
    tm = x.shape[0]
    ms = jnp.mean(x * x, axis=-1, keepdims=True)
    h = (x * lax.rsqrt(ms + RMS_EPS)) * g_ref[...]
    h = h * (1.0 + sc_ref[0]) + sh_ref[0]
    hb = h.astype(BF16)

    sg = jnp.dot(hb, wext_ref[:, 0:n_shift + 256], preferred_element_type=F32)
    o = n_shift + 256
    xg = jnp.dot(hb, wext_ref[:, o:o + 2 * b_width], preferred_element_type=F32)
    o += 2 * b_width

    def gate_strand():
        step = 2 * MXU_TILE
        for out, base in ((s1_out, o), (s2_out, o + d_model)):
            for c in range(0, d_model, step):
                out[0, :, c:c + step] = _sigmoid(jnp.dot(
                    hb, wext_ref[:, base + c:base + c + step], preferred_element_type=F32))
                yield

    def rwkv_strand():
        yield from _rwkv_heads(sg, n_shift, A, carry_ref, mu_ref, w0_ref, a0_ref, w2cat_ref,
                               kk_ref, ka_ref, rk_ref, bd_ref, r_out, lw_out, k_out, v_out,
                               kk_out, b_out, g_out, bonus_out)

    strands = [gate_strand(), rwkv_strand(),
               _rglru_tile(xg[:, 0:b_width], xg[:, b_width:2 * b_width], cw_ref, cb_ref, wr_ref,
                           br_ref, wi_ref, bi_ref, lam_ref, xc_carry, h_carry, yb_out)]
    while strands:
        strands = [c for c in strands if next(c, "done") != "done"]


def _rwkv_heads(sg, n_shift, A, carry_ref, mu_ref, w0_ref, a0_ref, w2cat_ref, kk_ref, ka_ref,
                rk_ref, bd_ref, r_out, lw_out, k_out, v_out, kk_out, b_out, g_out, bonus_out):
    tm = sg.shape[0]
    cur = sg[:, 0:n_shift]
    row = lax.broadcasted_iota(jnp.int32, (tm, 1), 0)
    prev = jnp.where(row == 0, carry_ref[0:1, :], pltpu.roll(cur, 1, axis=0))
    carry_ref[0:1, :] = cur[tm - 1:tm, :]

    rkv = cur[:, 0:3 * A]
    rkv = rkv + (prev[:, 0:3 * A] - rkv) * mu_ref[...]
    r = rkv[:, 0:A]
    k = rkv[:, A:2 * A]
    v = rkv[:, 2 * A:3 * A]
    r_out[0] = r
    v_out[0] = v
    yield

    pre = sg[:, n_shift:n_shift + 256] + prev[:, 3 * A:3 * A + 256]
    lane = lax.broadcasted_iota(jnp.int32, pre.shape, 1)
    act = jnp.where(lane < 64, jnp.tanh(pre), jnp.where(lane < 128, pre, _sigmoid(pre)))
    low = _dot(act, w2cat_ref[...])
    bd = bd_ref[...]
    kkr = k * kk_ref[...]
    kk_ss = _head_sums(kkr * kkr, bd)
    yield
    w_log = -_softplus(-(w0_ref[...] + low[:, 0:A])) - 0.5
    lw_out[0] = -jnp.exp(w_log)
    g_out[0] = low[:, 2 * A:3 * A]
    yield
    iclr = _sigmoid(a0_ref[...] + low[:, A:2 * A])
    kkn = kkr * lax.rsqrt(kk_ss + 1e-12)
    k2 = k * (1.0 + (iclr - 1.0) * ka_ref[...])
    k_out[0] = k2
    kk_out[0] = kkn
    b_out[0] = kkn * iclr
    yield
    bonus_out[0] = _head_sums(r * k2 * rk_ref[...], bd) * v


def _inproj(x, sh1, sc1, g_mix, wext, mu_rkv, w0, a0, w2cat, k_k, k_a, r_k, bd,
            conv_w, conv_b, wr, br, wi, bi, lam, *, tm=256):
    B, S, D = x.shape
    A = w0.shape[-1]
    Bw = lam.shape[-1]
    row = lambda a: a.reshape(1, -1)
    full = lambda a: pl.BlockSpec(a.shape, lambda b, s: (0,) * a.ndim)
    tok = lambda n: pl.BlockSpec((1, tm, n), lambda b, s: (b, s, 0))
    per_b = pl.BlockSpec((1, 1, D), lambda b, s: (b, 0, 0))
    small = [row(g_mix), wext, row(mu_rkv), row(w0), row(a0), w2cat, row(k_k), row(k_a),
             row(r_k), bd, conv_w, row(conv_b), wr, row(br), wi, row(bi), row(lam)]
    outs = [jax.ShapeDtypeStruct((B, S, A), F32)] * 8 + \
           [jax.ShapeDtypeStruct((B, S, Bw), F32)] + \
           [jax.ShapeDtypeStruct((B, S, D), F32)] * 2
    return pl.pallas_call(
        functools.partial(_inproj_kernel, a_width=A, b_width=Bw, d_model=D),
        grid=(B, S // tm),
        in_specs=[tok(D), per_b, per_b] + [full(a) for a in small],
        out_specs=[tok(A)] * 8 + [tok(Bw)] + [tok(D)] * 2,
        out_shape=outs,
        scratch_shapes=[pltpu.VMEM((8, 3 * A + 256), F32), pltpu.VMEM((8, Bw), F32),
                        pltpu.VMEM((8, Bw), F32)],
        compiler_params=_cparams(2),
        name="inproj",
    )(x, sh1.reshape(B, 1, D), sc1.reshape(B, 1, D), *small)


def _rwkv_kernel(r_ref, lw_ref, k_ref, v_ref, kk_ref, b_ref, tri_ref, bdm_ref, slm_ref,
                 lm_ref, dgm_ref, bkm_ref, y_ref, s_ref):
    @pl.when(pl.program_id(0) == 0)
    def _():
        s_ref[...] = jnp.zeros_like(s_ref)

    W = QUAD * HEAD
    n_b, rows, n_a = r_ref.shape
    masks = (tri_ref[...], bdm_ref[...], slm_ref[...], lm_ref[...], dgm_ref[...], bkm_ref[...])
    chains = []
    state_version = {}
    for c in range(rows // CHUNK):
        t = slice(c * CHUNK, (c + 1) * CHUNK)
        for bi in range(n_b):
            for q in range(n_a // W):
                cols = slice(q * W, (q + 1) * W)
                idx = bi * (n_a // W) + q
                ins = [ref[bi, t, cols] for ref in (r_ref, lw_ref, k_ref, v_ref, kk_ref, b_ref)]
                chains.append(_rwkv_chunk(*ins, s_ref, idx, masks, state_version, c,
                                          functools.partial(_store_y, y_ref, bi, t, cols)))
    while chains:
        chains = [c for c in chains if next(c, "done") != "done"]


def _store_y(y_ref, bi, t, cols, y):
    y_ref[bi, t, cols] = y


def _rwkv_chunk(r, lw, k, v, kk, b, s_ref, idx, masks, state_version, version, store_y):
    tri, bdm, slm, lm, dgm, bkm = masks
    L = CHUNK
    a = -kk

    cl = _dot_hl3(tri, lw)
    clp = cl - lw
    cm = cl[L // 2 - 1:L // 2, :]
    ce = cl[L - 1:L, :]
    at = a * jnp.exp(clp - cm)
    rt = r * jnp.exp(cl - cm)
    e_inv = jnp.exp(cm - cl)
    bt = b * e_inv
    kt = k * e_inv
    e_end = jnp.exp(ce - cl)
    bh = b * e_end
    kh = k * e_end
    g_mid = jnp.exp(cm)
    g_end = jnp.exp(ce)

    tile = lambda m: jnp.concatenate([m] * QUAD, axis=0)
    stack = lambda m: tile(m) * bdm
    a_s = stack(at)
    r_s = stack(rt)
    v_s = stack(v)
    bh_s = stack(bh)
    kh_s = stack(kh)
    bt_t = tile(bt)
    kt_t = tile(kt)
    yield

    a_ab = _dot(a_s, bt_t, NT) * slm
    a_ak = _dot(a_s, kt_t, NT) * slm
    a_rb = _dot(r_s, bt_t, NT) * lm
    a_rk = _dot(r_s, kt_t, NT) * lm
    yield

    d = a_ab * dgm
    e = a_ab - d
    n_blk = d.shape[0] // SUB
    fold = lambda m: functools.reduce(
        lambda acc, b: acc + m[b * SUB:(b + 1) * SUB], range(1, n_blk), m[0:SUB])
    unfold = lambda strip: jnp.concatenate([strip] * n_blk, axis=0) * bkm
    p_full = d
    p_strip = fold(d)
    d_inv_strip = fold(lm - slm) + p_strip
    n = 2
    while n < SUB:
        p_strip = _dot(p_strip, p_full)
        yield
        p_full = unfold(p_strip)
        d_inv_strip = d_inv_strip + _dot(d_inv_strip, p_full)
        yield
        n *= 2
    d_inv = unfold(d_inv_strip)
    f = _dot(d_inv, e)
    yield
    t_inv = d_inv
    fp = f
    n = 1
    terms = []
    while n < L // SUB:
        terms.append(fp)
        n *= 2
        if n < L // SUB:
            fp = _dot(fp, fp)
            yield
    for fp in reversed(terms):
        t_inv = t_inv + _dot(fp, t_inv)
        yield

    while state_version.get(idx, 0) != version:
        yield
    s0 = s_ref[idx]
    s_mid = s0 * g_mid
    rhs = _dot(a_s, s_mid, NT) + _dot(a_ak, v_s)
    y_part = _dot(r_s, s_mid, NT) + _dot(a_rk, v_s)
    yield
    u = _dot(t_inv, rhs)
    yield
    y_s = y_part + _dot(a_rb, u)
    s_ref[idx] = s0 * g_end + _dot(u, bh_s, TN) + _dot(v_s, kh_s, TN)
    state_version[idx] = version + 1
    y = y_s[0:L]
    for q in range(1, QUAD):
        y = y + y_s[q * L:(q + 1) * L]
    store_y(y)


def _dot_hl3(tri, lw):
    hi = lw.astype(BF16)
    r1 = lw - hi.astype(F32)
    mid = r1.astype(BF16)
    lo = (r1 - mid.astype(F32)).astype(BF16)
    d = functools.partial(jnp.dot, preferred_element_type=F32)
    return d(tri, hi) + (d(tri, mid) + d(tri, lo))


def _rwkv(r, lw, k, v, kk, b):
    B, S, A = r.shape
    W = QUAD * HEAD
    L = CHUNK
    idx = jnp.arange(W)
    assert L == HEAD, "stacked rows (head, time) and lanes (head, channel) share one block size"
    same = (idx[:, None] // L) == (idx[None, :] // L)
    ti = idx[:, None] % L
    tj = idx[None, :] % L
    bdm = same.astype(F32)
    slm = (same & (ti > tj)).astype(F32)
    lm = (same & (ti >= tj)).astype(F32)
    bkm = (same & (ti // SUB == tj // SUB)).astype(F32)
    dgm = (same & (ti > tj) & (ti // SUB == tj // SUB)).astype(F32)
    tri = (jnp.arange(L)[:, None] >= jnp.arange(L)[None, :]).astype(BF16)
    rows = RWKV_STEP_CHUNKS * L
    tok = pl.BlockSpec((B, rows, A), lambda c: (0, c, 0))
    full = lambda a: pl.BlockSpec(a.shape, lambda c: (0,) * a.ndim)
    return pl.pallas_call(
        _rwkv_kernel,
        grid=(S // rows,),
        in_specs=[tok] * 6 + [full(a) for a in (tri, bdm, slm, lm, dgm, bkm)],
        out_specs=tok,
        out_shape=jax.ShapeDtypeStruct((B, S, A), F32),
        scratch_shapes=[pltpu.VMEM((B * (A // W), W, W), F32)],
        compiler_params=_cparams(1),
        name="rwkv7",
    )(r, lw, k, v, kk, b, tri, bdm, slm, lm, dgm, bkm)


def _wide_block_dot(a, w_ref):
    w = w_ref.shape[-1]
    parts = [_dot(a[:, i * w:(i + 1) * w], w_ref[i]) for i in range(w_ref.shape[0])]
    return parts[0] if len(parts) == 1 else jnp.concatenate(parts, axis=1)


def _rglru_tile(xb, gb, cw_ref, cb_ref, wr_ref, br_ref, wi_ref, bi_ref, lam_ref,
                xc_carry, h_carry, y_out):
    tt, width = xb.shape
    ext = jnp.concatenate([xc_carry[...], xb], axis=0)
    xc_carry[...] = xb[tt - 8:tt, :]
    xc = cb_ref[...] + cw_ref[CONV_WIDTH - 1:CONV_WIDTH, :] * xb
    for d in range(1, CONV_WIDTH):
        xc = xc + cw_ref[CONV_WIDTH - 1 - d:CONV_WIDTH - d, :] * ext[8 - d:8 - d + tt, :]

    pre_r = _wide_block_dot(xc, wr_ref)
    pre_i = _wide_block_dot(xc, wi_ref)
    yield
    gate_r = _sigmoid(pre_r + br_ref[...])
    gate_i = _sigmoid(pre_i + bi_ref[...])
    log_a = -LRU_C * gate_r * _softplus(-lam_ref[...])
    a = jnp.exp(log_a)
    u = jnp.sqrt(-jnp.tanh(log_a) * (a * a + 1.0)) * (gate_i * xc)
    yield

    a3 = a.reshape(tt // 8, 8, width)
    u3 = u.reshape(tt // 8, 8, width)
    sub = lax.broadcasted_iota(jnp.int32, (1, 8, 1), 1)
    d = 1
    while d < 8:
        keep = sub >= d
        a_sh = jnp.where(keep, pltpu.roll(a3, d, axis=1), 1.0)
        u_sh = jnp.where(keep, pltpu.roll(u3, d, axis=1), 0.0)
        u3 = a3 * u_sh + u3
        a3 = a3 * a_sh
        d *= 2
        yield
    gate = _gelu_tanh(gb)
    carry = h_carry[0:1, :]
    groups = []
    for i in range(tt // 8):
        hg = a3[i] * carry + u3[i]
        groups.append(hg)
        carry = hg[7:8, :]
        if i % 8 == 7:
            yield
    h_carry[0:1, :] = carry
    y_out[0] = jnp.concatenate(groups, axis=0) * gate


def _merge_kernel(x_ref, ya_ref, bonus_ref, g_ref, yb_ref, s1_ref, s2_ref, gt1_ref, sh2_ref,
                  sc2_ref, lng_ref, lnb_ref, bd_ref, pa_ref, pb_ref, wout_ref, gffn_ref,
                  wr_ref, br_ref, x1_out, h2_out, logit_out):
    ya = ya_ref[0]
    bd = bd_ref[...]
    inv_n = 1.0 / HEAD
    mean = _head_sums(ya, bd) * inv_n
    yc = ya - mean
    var = _head_sums(yc * yc, bd) * inv_n
    yn = yc * lax.rsqrt(var + GN_EPS) * lng_ref[...] + lnb_ref[...]
    ya2 = (yn + bonus_ref[0]) * g_ref[0]

    merged = s1_ref[0] * _dot(ya2, pa_ref[...]) + s2_ref[0] * _dot(yb_ref[0], pb_ref[...])
    x1 = x_ref[0] + gt1_ref[0] * _dot(merged, wout_ref[...])
    x1_out[0] = x1

    ms = jnp.mean(x1 * x1, axis=-1, keepdims=True)
    h2 = (x1 * lax.rsqrt(ms + RMS_EPS)) * gffn_ref[...]
    h2 = h2 * (1.0 + sc2_ref[0]) + sh2_ref[0]
    h2_out[0] = h2.astype(BF16)

    logit_out[0] = _dot3(h2, wr_ref[...]) + br_ref[...]


def _router_kernel(logit_ref, tri_ref, utri_ref, meta_out, metat_out, seg_out):
    logits = logit_ref[0]
    lane = lax.broadcasted_iota(jnp.int32, logits.shape, 1).astype(F32)
    neg = jnp.float32(-jnp.inf)
    big = jnp.float32(1e9)
    is_g = lane < N_GROUPS
    lg = jnp.where(is_g, logits, neg)
    mg = jnp.max(lg, axis=-1, keepdims=True)
    pg_top = 1.0 / jnp.sum(jnp.exp(lg - mg), axis=-1, keepdims=True)
    g_idx = jnp.min(jnp.where(lg == mg, lane, big), axis=-1, keepdims=True)
    e_lane = lane - N_GROUPS
    in_grp = (e_lane >= g_idx * EXPERTS_PER_GROUP) & (e_lane < (g_idx + 1) * EXPERTS_PER_GROUP)
    le = jnp.where(in_grp, logits, neg)
    me = jnp.max(le, axis=-1, keepdims=True)
    i1 = jnp.min(jnp.where(le == me, lane, big), axis=-1, keepdims=True)
    ee = jnp.exp(le - me)
    se = jnp.sum(ee, axis=-1, keepdims=True)
    rest = jnp.where(lane != i1, le, neg)
    m2 = jnp.max(rest, axis=-1, keepdims=True)
    i2 = jnp.min(jnp.where(rest == m2, lane, big), axis=-1, keepdims=True)
    p1 = 1.0 / se
    p2 = jnp.exp(m2 - me) / se
    den = p1 + p2
    wt1 = pg_top * (p1 / den)
    wt2 = pg_top * (p2 / den)

    hot1 = jnp.where(lane == i1, 1.0, 0.0)
    hot2 = jnp.where(lane == i2, 1.0, 0.0)
    both = hot1 + hot2
    n_sub = logits.shape[0] // MOE_SUB
    pos1, pos2, seg_rows = [], [], []
    for s in range(n_sub):
        rows = slice(s * MOE_SUB, (s + 1) * MOE_SUB)
        before = _dot(tri_ref[...], both[rows])
        count = jnp.sum(both[rows], axis=0, keepdims=True)
        padded = jnp.floor((count + (MOE_ALIGN - 1)) * (1.0 / MOE_ALIGN)) * MOE_ALIGN
        start = _dot_hl(jnp.broadcast_to(padded, (8, padded.shape[1])), utri_ref[...])[0:1] \
            + float(s * MOE_CAP)
        slot = before + start
        pos1.append(jnp.sum(hot1[rows] * slot, axis=-1, keepdims=True))
        pos2.append(jnp.sum(hot2[rows] * slot, axis=-1, keepdims=True))
        seg_rows.append((start, count))
    pos1 = jnp.concatenate(pos1, axis=0)
    pos2 = jnp.concatenate(pos2, axis=0)
    meta = jnp.where(lane == 0, pos1, jnp.where(lane == 1, pos2,
                     jnp.where(lane == 2, wt1, jnp.where(lane == 3, wt2, 0.0))))
    meta_out[0] = meta
    metat_out[0] = jnp.transpose(meta)[0:8, :]
    seg = [st for st, _ in seg_rows] + [ct for _, ct in seg_rows]
    seg_out[0, 0] = jnp.concatenate(seg, axis=0)


def _router(logits, *, tm):
    B, S, n = logits.shape
    assert 2 * (tm // MOE_SUB) == 8, "segment table holds one start and one count row per sub-tile"
    tri = (jnp.arange(MOE_SUB)[:, None] > jnp.arange(MOE_SUB)[None, :]).astype(BF16)
    utri = (jnp.arange(n)[:, None] < jnp.arange(n)[None, :]).astype(BF16)
    tok = pl.BlockSpec((1, tm, n), lambda b, s: (b, s, 0))
    full = lambda a: pl.BlockSpec(a.shape, lambda b, s: (0,) * a.ndim)
    return pl.pallas_call(
        _router_kernel,
        grid=(B, S // tm),
        in_specs=[tok, full(tri), full(utri)],
        out_specs=[tok, pl.BlockSpec((1, 8, tm), lambda b, s: (b, 0, s)),
                   pl.BlockSpec((1, 1, 8, n), lambda b, s: (b, s, 0, 0))],
        out_shape=[jax.ShapeDtypeStruct((B, S, n), F32),
                   jax.ShapeDtypeStruct((B, 8, S), F32),
                   jax.ShapeDtypeStruct((B, S // tm, 8, n), F32)],
        compiler_params=_cparams(2),
        name="router",
    )(logits, tri, utri)


def _merge(x, ya, bonus, g, yb, s1, s2, gt1, sh2, sc2, lnx_g, lnx_b, bd, pa, pb, wout, g_ffn,
           wr, br, *, tm=512):
    B, S, D = x.shape
    A = ya.shape[-1]
    n = wr.shape[1]
    row = lambda a: a.reshape(1, -1)
    full = lambda a: pl.BlockSpec(a.shape, lambda b, s: (0,) * a.ndim)
    tok = lambda n: pl.BlockSpec((1, tm, n), lambda b, s: (b, s, 0))
    per_b = pl.BlockSpec((1, 1, D), lambda b, s: (b, 0, 0))
    small = [row(lnx_g), row(lnx_b), bd, pa, pb, wout, row(g_ffn), wr, row(br)]
    return pl.pallas_call(
        _merge_kernel,
        grid=(B, S // tm),
        in_specs=[tok(D), tok(A), tok(A), tok(A), tok(A), tok(D), tok(D), per_b, per_b, per_b]
        + [full(a) for a in small],
        out_specs=[tok(D), tok(D), tok(n)],
        out_shape=[jax.ShapeDtypeStruct((B, S, D), F32),
                   jax.ShapeDtypeStruct((B, S, D), BF16),
                   jax.ShapeDtypeStruct((B, S, n), F32)],
        compiler_params=_cparams(2),
        name="merge",
    )(x, ya, bonus, g, yb, s1, s2, gt1.reshape(B, 1, D), sh2.reshape(B, 1, D),
      sc2.reshape(B, 1, D), *small)


def _moe_kernel(seg, x1_ref, h2_ref, meta_ref, metat_ref, w1_ref, w3_ref, w2_ref,
                gt2_ref, gf_ref, shf_ref, scf_ref, o_ref, buf_ref, hid_ref):
    b = pl.program_id(0)
    i = pl.program_id(1)
    g = pl.program_id(2)
    n_g = pl.num_programs(2)
    tm = h2_ref.shape[1]
    n_sub = tm // MOE_SUB
    W = MOE_WINDOW
    base = (b * pl.num_programs(1) + i) * (2 * n_sub * N_EXPERTS)

    def local(slots, s):
        return slots[:, s * MOE_SUB:(s + 1) * MOE_SUB] - float(s * MOE_CAP)

    pos1_row = metat_ref[0, 0:1, :]
    pos2_row = metat_ref[0, 1:2, :]
    wt1_row = metat_ref[0, 2:3, :]
    wt2_row = metat_ref[0, 3:4, :]

    @pl.when(g == 0)
    def _():
        blk = MXU_TILE
        for s in range(n_sub):
            p1 = local(pos1_row, s)
            p2 = local(pos2_row, s)
            h2s = h2_ref[0, s * MOE_SUB:(s + 1) * MOE_SUB, :]
            for r0 in range(0, MOE_CAP, blk):
                rid = lax.broadcasted_iota(jnp.int32, (blk, 1), 0).astype(F32) + float(r0)
                hot = jnp.where((p1 == rid) | (p2 == rid), 1.0, 0.0).astype(BF16)
                buf_ref[s * MOE_CAP + r0:s * MOE_CAP + r0 + blk, :] = jnp.dot(
                    hot, h2s, preferred_element_type=F32).astype(BF16)

    wrow = lax.broadcasted_iota(jnp.int32, (W, 1), 0)
    ge0 = g * EXPERTS_PER_GROUP
    first = [jnp.int32(0)]
    for e in range(EXPERTS_PER_GROUP):
        n_max = functools.reduce(
            jnp.maximum, [seg[base + (n_sub + s) * N_EXPERTS + ge0 + e] for s in range(n_sub)])
        first.append(first[-1] + (n_max + W - 1) // W)
    total = first[-1]
    n_f = w1_ref.shape[-1]

    def locate(k):
        inside = jnp.logical_and(k >= 0, k < total)
        e = functools.reduce(lambda a, c: a + (k >= c).astype(jnp.int32),
                             first[1:EXPERTS_PER_GROUP], jnp.int32(0))
        w = k - functools.reduce(lambda a, ec: jnp.where(e == ec[0], ec[1], a),
                                 enumerate(first[:-1]), jnp.int32(0))
        e = jnp.where(inside, e, EXPERTS_PER_GROUP - 1)
        w = jnp.where(inside, w, MOE_CAP // W)
        starts = [seg[base + s * N_EXPERTS + ge0 + e] for s in range(n_sub)]
        counts = [seg[base + (n_sub + s) * N_EXPERTS + ge0 + e] for s in range(n_sub)]
        offs = [pl.multiple_of(jnp.minimum(starts[s] + w * W, (s + 1) * MOE_CAP - W), MOE_ALIGN)
                for s in range(n_sub)]
        return e, w, offs, starts, counts

    def read_windows(it):
        out = []
        for j in range(MOE_ITEMS):
            e, w, offs, starts, counts = locate(it * MOE_ITEMS + j)
            out.append((e, jnp.concatenate(
                [buf_ref[pl.ds(offs[s], W), :] for s in range(n_sub)], axis=0)))
        return out

    def project_up(windows):
        for j, (e, xcat) in enumerate(windows):
            hid_ref[j, :, 0:n_f] = jnp.dot(xcat, w1_ref[e], preferred_element_type=F32)
            hid_ref[j, :, n_f:2 * n_f] = jnp.dot(xcat, w3_ref[e], preferred_element_type=F32)

    def finish(it, hidden):
        for j in range(MOE_ITEMS):
            e, w, offs, starts, counts = locate(it * MOE_ITEMS + j)
            hid = hidden[j]
            he = _silu(hid[:, 0:n_f]) * hid[:, n_f:2 * n_f]
            y = _dot(he, w2_ref[e])
            for s in range(n_sub):
                rid = (starts[s] + w * W + wrow).astype(F32)
                cols = slice(s * MOE_SUB, (s + 1) * MOE_SUB)
                wt = jnp.sum(jnp.where(pos1_row[:, cols] == rid, wt1_row[:, cols], 0.0)
                             + jnp.where(pos2_row[:, cols] == rid, wt2_row[:, cols], 0.0),
                             axis=-1, keepdims=True)
                valid = wrow < counts[s] - w * W
                new = (wt * y[s * W:(s + 1) * W]).astype(BF16)
                old = buf_ref[pl.ds(offs[s], W), :]
                buf_ref[pl.ds(offs[s], W), :] = jnp.where(valid, new, old)

    def pipeline(it, carry):
        windows = read_windows(it)
        hidden = [hid_ref[j] for j in range(MOE_ITEMS)]
        project_up(windows)
        finish(it - 1, hidden)
        return carry

    n_iter = (total + MOE_ITEMS - 1) // MOE_ITEMS

    @pl.when(total > 0)
    def _():
        project_up(read_windows(0))

    lax.fori_loop(1, n_iter, pipeline, 0)

    @pl.when(total > 0)
    def _():
        finish(n_iter - 1, [hid_ref[j] for j in range(MOE_ITEMS)])

    @pl.when(g == n_g - 1)
    def _():
        meta = meta_ref[0]
        slot_ids = lax.broadcasted_iota(jnp.int32, (1, MOE_CAP), 1).astype(F32)
        for s in range(n_sub):
            rows = slice(s * MOE_SUB, (s + 1) * MOE_SUB)
            p1 = meta[rows, 0:1] - float(s * MOE_CAP)
            p2 = meta[rows, 1:2] - float(s * MOE_CAP)
            hot = jnp.where((p1 == slot_ids) | (p2 == slot_ids), 1.0, 0.0).astype(BF16)
            moe = jnp.dot(hot, buf_ref[s * MOE_CAP:(s + 1) * MOE_CAP, :],
                          preferred_element_type=F32)
            x2 = x1_ref[0, rows, :] + gt2_ref[0] * moe
            ms = jnp.mean(x2 * x2, axis=-1, keepdims=True)
            y = (x2 * lax.rsqrt(ms + RMS_EPS)) * gf_ref[...]
            o_ref[0, rows, :] = y * (1.0 + scf_ref[0]) + shf_ref[0]


def _moe(seg, x1, h2, meta, metat, w1, w3, w2, gt2, g_final, shf, scf, *, tm):
    B, S, D = x1.shape
    E, _, F = w1.shape
    G = EXPERTS_PER_GROUP
    assert E == N_EXPERTS and MOE_CAP % MXU_TILE == 0 and tm % MOE_SUB == 0
    tok = lambda n: pl.BlockSpec((1, tm, n), lambda b, s, g, c: (b, s, 0))
    per_b = pl.BlockSpec((1, 1, D), lambda b, s, g, c: (b, 0, 0))
    grid_spec = pltpu.PrefetchScalarGridSpec(
        num_scalar_prefetch=1,
        grid=(B, S // tm, E // G),
        in_specs=[tok(D), tok(D), tok(128),
                  pl.BlockSpec((1, 8, tm), lambda b, s, g, c: (b, 0, s)),
                  pl.BlockSpec((G, D, F), lambda b, s, g, c: (g, 0, 0)),
                  pl.BlockSpec((G, D, F), lambda b, s, g, c: (g, 0, 0)),
                  pl.BlockSpec((G, F, D), lambda b, s, g, c: (g, 0, 0)),
                  per_b, pl.BlockSpec((1, D), lambda b, s, g, c: (0, 0)), per_b, per_b],
        out_specs=tok(D),
        scratch_shapes=[pltpu.VMEM(((tm // MOE_SUB) * MOE_CAP, D), BF16),
                        pltpu.VMEM((MOE_ITEMS, (tm // MOE_SUB) * MOE_WINDOW, 2 * F), F32)])
    return pl.pallas_call(
        _moe_kernel,
        grid_spec=grid_spec,
        out_shape=jax.ShapeDtypeStruct((B, S, D), F32),
        compiler_params=_cparams(3),
        name="moe",
    )(seg, x1, h2, meta, metat, w1, w3, w2, gt2.reshape(B, 1, D), g_final.reshape(1, D),
      shf.reshape(B, 1, D), scf.reshape(B, 1, D))


def _block_diag(w):
    G, n, _ = w.shape
    eye = jnp.eye(G, dtype=w.dtype)
    return (eye[:, None, :, None] * w[:, :, None, :]).reshape(G * n, G * n)


def kernel(x, c, w_ada, b_ada, g_mix, w_in, mu_rkv, mu_wag, w0, w1, w2, a0, a1, a2, g1, g2, k_k, k_a, r_k, lnx_g, lnx_b, conv_w, conv_b, w_rgate, b_rgate, w_igate, b_igate, lam, p_a, p_b, w_out, g_ffn, w_rg, b_rg, w_re, b_re, w1e, w3e, w2e, g_final, w_ada_f, b_ada_f):
    B, S, D = x.shape
    depth = w_ada.shape[0]
    assert depth == 1, "the final norm is fused into the last MoE call; one layer supported"
    A = w0.shape[-1]
    Bw = lam.shape[-1]

    c8 = jnp.zeros((8, D), F32).at[:B].set(c)
    modf = _ada(c8, w_ada_f, b_ada_f)[:B]
    shf, scf = modf[:, :D], modf[:, D:]

    head_ids = jnp.arange(QUAD * HEAD) // HEAD
    bd = (head_ids[:, None] == head_ids[None, :]).astype(BF16)

    for l in range(depth):
        mod = _ada(c8, w_ada[l], b_ada[l])[:B]
        sh1, sc1, gt1, sh2, sc2, gt2 = (mod[:, i * D:(i + 1) * D] for i in range(6))

        lowrank = jnp.concatenate([w1[l], a1[l], g1[l]], axis=1)
        mu_cols = jnp.concatenate([jnp.broadcast_to(mu_wag[l, 0][:, None], w1[l].shape),
                                   jnp.broadcast_to(mu_wag[l, 1][:, None], a1[l].shape),
                                   jnp.broadcast_to(mu_wag[l, 2][:, None], g1[l].shape)], axis=1)
        o1 = 3 * A
        wext = jnp.concatenate([w_in[l][:, :o1], mu_cols * lowrank, (1.0 - mu_cols) * lowrank,
                                w_in[l][:, o1:]], axis=1).astype(BF16)
        ra, rb = w2.shape[1], g2.shape[1]
        w2cat = jnp.zeros((2 * ra + rb, 3 * A), F32)
        w2cat = w2cat.at[0:ra, 0:A].set(w2[l]).at[ra:2 * ra, A:2 * A].set(a2[l])
        w2cat = w2cat.at[2 * ra:, 2 * A:].set(g2[l]).astype(BF16)

        per_tile = QUAD * HEAD // w_rgate.shape[-1]
        wide = lambda w: jax.vmap(_block_diag)(
            w.reshape(-1, per_tile, *w.shape[1:])).astype(BF16)
        (r, lw, k, v, kk, bvec, g, bonus, yb, s1, s2) = _inproj(
            x, sh1, sc1, g_mix[l], wext, mu_rkv[l], w0[l], a0[l], w2cat, k_k[l], k_a[l],
            r_k[l].reshape(-1), bd, conv_w[l].reshape(CONV_WIDTH, Bw), conv_b[l],
            wide(w_rgate[l]), b_rgate[l], wide(w_igate[l]), b_igate[l], lam[l])

        ya = _rwkv(r, lw, k, v, kk, bvec)

        n_g, n_e = w_rg.shape[-1], w_re.shape[-1]
        wr = jnp.zeros((D, 128), F32).at[:, :n_g].set(w_rg[l]).at[:, n_g:n_g + n_e].set(w_re[l])
        br = jnp.zeros((128,), F32).at[:n_g].set(b_rg[l]).at[n_g:n_g + n_e].set(b_re[l])
        x1, h2, logits = _merge(
            x, ya, bonus, g, yb, s1, s2, gt1, sh2, sc2, lnx_g[l], lnx_b[l], bd,
            p_a[l].astype(BF16), p_b[l].astype(BF16), w_out[l].astype(BF16),
            g_ffn[l], wr, br)
        meta, metat, seg = _router(logits, tm=MOE_TILE)
        seg = seg[:, :, :, n_g:n_g + n_e].astype(jnp.int32).reshape(-1)
        x = _moe(seg, x1, h2, meta, metat, w1e[l].astype(BF16), w3e[l].astype(BF16),
                 w2e[l].astype(BF16), gt2, g_final, shf, scf, tm=MOE_TILE)
    return x
```

```python
import functools

import jax
import jax.numpy as jnp
from jax import lax
from jax.experimental import pallas as pl
from jax.experimental.pallas import tpu as pltpu

F32 = jnp.float32
BF16 = jnp.bfloat16

MXU_TILE = 256
HEAD = 64
CHUNK = 64
SUB = 16
QUAD = 4
RWKV_STEP_CHUNKS = 4
RMS_EPS = 1e-6
GN_EPS = 64e-5
LRU_C = 8.0
CONV_WIDTH = 4
N_GROUPS = 4
EXPERTS_PER_GROUP = 8
N_EXPERTS = N_GROUPS * EXPERTS_PER_GROUP
MOE_TILE = 1024
MOE_SUB = 256
MOE_ALIGN = 16
MOE_WINDOW = 32
MOE_ITEMS = 4
MOE_CAP = 2 * MOE_SUB + N_EXPERTS * (MOE_ALIGN - 1) + MOE_WINDOW
VMEM_LIMIT = 58 * 1024 * 1024


def _cparams(n_axes):
    return pltpu.CompilerParams(dimension_semantics=("arbitrary",) * n_axes,
                                vmem_limit_bytes=VMEM_LIMIT)


def _dot(a, b, dims=(((1,), (0,)), ((), ()))):
    return lax.dot_general(a.astype(BF16), b.astype(BF16), dims,
                           preferred_element_type=F32)


NT = (((1,), (1,)), ((), ()))
TN = (((0,), (0,)), ((), ()))


def _split(a):
    hi = a.astype(BF16)
    lo = (a - hi.astype(F32)).astype(BF16)
    return hi, lo


def _dot_hl(a, b_exact, dims=(((1,), (0,)), ((), ()))):
    hi, lo = _split(a)
    return (lax.dot_general(hi, b_exact, dims, preferred_element_type=F32)
            + lax.dot_general(lo, b_exact, dims, preferred_element_type=F32))


def _head_sums(a, ones_blocks):
    w = ones_blocks.shape[0]
    parts = [_dot_hl(a[:, o:o + w], ones_blocks) for o in range(0, a.shape[1], w)]
    return parts[0] if len(parts) == 1 else jnp.concatenate(parts, axis=1)


def _dot3(a, b, dims=(((1,), (0,)), ((), ()))):
    ah, al = _split(a)
    bh, bl = _split(b)
    d = functools.partial(lax.dot_general, dimension_numbers=dims,
                          preferred_element_type=F32)
    return d(ah, bh) + (d(ah, bl) + d(al, bh))


def _sigmoid(z):
    return 0.5 * jnp.tanh(0.5 * z) + 0.5


def _softplus(z):
    return jnp.maximum(z, 0.0) + jnp.log1p(jnp.exp(-jnp.abs(z)))


def _silu(z):
    return z * _sigmoid(z)


def _gelu_tanh(z):
    return 0.5 * z * (1.0 + jnp.tanh(0.7978845608028654 * (z + 0.044715 * (z * z * z))))


def _ada_kernel(c_ref, w_ref, b_ref, o_ref):
    ca = _silu(c_ref[...])
    o_ref[...] = _dot3(ca, w_ref[...]) + b_ref[...]


def _ada(c8, w, b, tn=1024):
    d, n = w.shape
    return pl.pallas_call(
        _ada_kernel,
        grid=(n // tn,),
        in_specs=[pl.BlockSpec((8, d), lambda j: (0, 0)),
                  pl.BlockSpec((d, tn), lambda j: (0, j)),
                  pl.BlockSpec((1, tn), lambda j: (0, j))],
        out_specs=pl.BlockSpec((8, tn), lambda j: (0, j)),
        out_shape=jax.ShapeDtypeStruct((8, n), F32),
        compiler_params=_cparams(1),
        name="adaln",
    )(c8, w, b.reshape(1, n))


def _inproj_kernel(x_ref, sh_ref, sc_ref, g_ref, wext_ref, mu_ref, w0_ref, a0_ref,
                   w2cat_ref, kk_ref, ka_ref, rk_ref, bd_ref,
                   cw_ref, cb_ref, wr_ref, br_ref, wi_ref, bi_ref, lam_ref,
                   r_out, lw_out, k_out, v_out, kk_out, b_out, g_out, bonus_out,
                   yb_out, s1_out, s2_out, carry_ref, xc_carry, h_carry, *, a_width, b_width,
                   d_model):
    A = a_width
    n_shift = 3 * A + 256
    s = pl.program_id(1)

    @pl.when(s == 0)
    def _():
        carry_ref[...] = jnp.zeros_like(carry_ref)
        xc_carry[...] = jnp.zeros_like(xc_carry)
        h_carry[...] = jnp.zeros_like(h_carry)

    x = x_ref[0]
    tm = x.shape[0]
    ms = jnp.mean(x * x, axis=-1, keepdims=True)
    h = (x * lax.rsqrt(ms + RMS_EPS)) * g_ref[...]
    h = h * (1.0 + sc_ref[0]) + sh_ref[0]
    hb = h.astype(BF16)

    sg = jnp.dot(hb, wext_ref[:, 0:n_shift + 256], preferred_element_type=F32)
    o = n_shift + 256
    xg = jnp.dot(hb, wext_ref[:, o:o + 2 * b_width], preferred_element_type=F32)
    o += 2 * b_width

    def gate_strand():
        step = 2 * MXU_TILE
        for out, base in ((s1_out, o), (s2_out, o + d_model)):
            for c in range(0, d_model, step):
                out[0, :, c:c + step] = _sigmoid(jnp.dot(
                    hb, wext_ref[:, base + c:base + c + step], preferred_element_type=F32))
                yield

    def rwkv_strand():
        yield from _rwkv_heads(sg, n_shift, A, carry_ref, mu_ref, w0_ref, a0_ref, w2cat_ref,
                               kk_ref, ka_ref, rk_ref, bd_ref, r_out, lw_out, k_out, v_out,
                               kk_out, b_out, g_out, bonus_out)

    strands = [gate_strand(), rwkv_strand(),
               _rglru_tile(xg[:, 0:b_width], xg[:, b_width:2 * b_width], cw_ref, cb_ref, wr_ref,
                           br_ref, wi_ref, bi_ref, lam_ref, xc_carry, h_carry, yb_out)]
    while strands:
        strands = [c for c in strands if next(c, "done") != "done"]


def _rwkv_heads(sg, n_shift, A, carry_ref, mu_ref, w0_ref, a0_ref, w2cat_ref, kk_ref, ka_ref,
                rk_ref, bd_ref, r_out, lw_out, k_out, v_out, kk_out, b_out, g_out, bonus_out):
    tm = sg.shape[0]
    cur = sg[:, 0:n_shift]
    row = lax.broadcasted_iota(jnp.int32, (tm, 1), 0)
    prev = jnp.where(row == 0, carry_ref[0:1, :], pltpu.roll(cur, 1, axis=0))
    carry_ref[0:1, :] = cur[tm - 1:tm, :]

    rkv = cur[:, 0:3 * A]
    rkv = rkv + (prev[:, 0:3 * A] - rkv) * mu_ref[...]
    r = rkv[:, 0:A]
    k = rkv[:, A:2 * A]
    v = rkv[:, 2 * A:3 * A]
    r_out[0] = r
    v_out[0] = v
    yield

    pre = sg[:, n_shift:n_shift + 256] + prev[:, 3 * A:3 * A + 256]
    lane = lax.broadcasted_iota(jnp.int32, pre.shape, 1)
    act = jnp.where(lane < 64, jnp.tanh(pre), jnp.where(lane < 128, pre, _sigmoid(pre)))
    low = _dot(act, w2cat_ref[...])
    bd = bd_ref[...]
    kkr = k * kk_ref[...]
    kk_ss = _head_sums(kkr * kkr, bd)
    yield
    w_log = -_softplus(-(w0_ref[...] + low[:, 0:A])) - 0.5
    lw_out[0] = -jnp.exp(w_log)
    g_out[0] = low[:, 2 * A:3 * A]
    yield
    iclr = _sigmoid(a0_ref[...] + low[:, A:2 * A])
    kkn = kkr * lax.rsqrt(kk_ss + 1e-12)
    k2 = k * (1.0 + (iclr - 1.0) * ka_ref[...])
    k_out[0] = k2
    kk_out[0] = kkn
    b_out[0] = kkn * iclr
    yield
    bonus_out[0] = _head_sums(r * k2 * rk_ref[...], bd) * v


def _inproj(x, sh1, sc1, g_mix, wext, mu_rkv, w0, a0, w2cat, k_k, k_a, r_k, bd,
            conv_w, conv_b, wr, br, wi, bi, lam, *, tm=512):
    B, S, D = x.shape
    A = w0.shape[-1]
    Bw = lam.shape[-1]
    row = lambda a: a.reshape(1, -1)
    full = lambda a: pl.BlockSpec(a.shape, lambda b, s: (0,) * a.ndim,
                                  pipeline_mode=pl.Buffered(1))
    tok = lambda n: pl.BlockSpec((1, tm, n), lambda b, s: (b, s, 0))
    per_b = pl.BlockSpec((1, 1, D), lambda b, s: (b, 0, 0))
    small = [row(g_mix), wext, row(mu_rkv), row(w0), row(a0), w2cat, row(k_k), row(k_a),
             row(r_k), bd, conv_w, row(conv_b), wr, row(br), wi, row(bi), row(lam)]
    outs = [jax.ShapeDtypeStruct((B, S, A), F32)] * 8 + \
           [jax.ShapeDtypeStruct((B, S, Bw), F32)] + \
           [jax.ShapeDtypeStruct((B, S, D), F32)] * 2
    return pl.pallas_call(
        functools.partial(_inproj_kernel, a_width=A, b_width=Bw, d_model=D),
        grid=(B, S // tm),
        in_specs=[tok(D), per_b, per_b] + [full(a) for a in small],
        out_specs=[tok(A)] * 8 + [tok(Bw)] + [tok(D)] * 2,
        out_shape=outs,
        scratch_shapes=[pltpu.VMEM((8, 3 * A + 256), F32), pltpu.VMEM((8, Bw), F32),
                        pltpu.VMEM((8, Bw), F32)],
        compiler_params=_cparams(2),
        name="inproj",
    )(x, sh1.reshape(B, 1, D), sc1.reshape(B, 1, D), *small)


def _rwkv_kernel(r_ref, lw_ref, k_ref, v_ref, kk_ref, b_ref, tri_ref, bdm_ref, slm_ref,
                 lm_ref, dgm_ref, bkm_ref, y_ref, s_ref):
    @pl.when(pl.program_id(0) == 0)
    def _():
        s_ref[...] = jnp.zeros_like(s_ref)

    W = QUAD * HEAD
    n_b, rows, n_a = r_ref.shape
    masks = (tri_ref[...], bdm_ref[...], slm_ref[...], lm_ref[...], dgm_ref[...], bkm_ref[...])
    chains = []
    state_version = {}
    for c in range(rows // CHUNK):
        t = slice(c * CHUNK, (c + 1) * CHUNK)
        for bi in range(n_b):
            for q in range(n_a // W):
                cols = slice(q * W, (q + 1) * W)
                idx = bi * (n_a // W) + q
                ins = [ref[bi, t, cols] for ref in (r_ref, lw_ref, k_ref, v_ref, kk_ref, b_ref)]
                chains.append(_rwkv_chunk(*ins, s_ref, idx, masks, state_version, c,
                                          functools.partial(_store_y, y_ref, bi, t, cols)))
    while chains:
        chains = [c for c in chains if next(c, "done") != "done"]


def _store_y(y_ref, bi, t, cols, y):
    y_ref[bi, t, cols] = y


def _rwkv_chunk(r, lw, k, v, kk, b, s_ref, idx, masks, state_version, version, store_y):
    tri, bdm, slm, lm, dgm, bkm = masks
    L = CHUNK
    a = -kk

    cl = _dot_hl3(tri, lw)
    clp = cl - lw
    cm = cl[L // 2 - 1:L // 2, :]
    ce = cl[L - 1:L, :]
    at = a * jnp.exp(clp - cm)
    rt = r * jnp.exp(cl - cm)
    e_inv = jnp.exp(cm - cl)
    bt = b * e_inv
    kt = k * e_inv
    e_end = jnp.exp(ce - cl)
    bh = b * e_end
    kh = k * e_end
    g_mid = jnp.exp(cm)
    g_end = jnp.exp(ce)

    tile = lambda m: jnp.concatenate([m] * QUAD, axis=0)
    stack = lambda m: tile(m) * bdm
    a_s = stack(at)
    r_s = stack(rt)
    v_s = stack(v)
    bh_s = stack(bh)
    kh_s = stack(kh)
    bt_t = tile(bt)
    kt_t = tile(kt)
    yield

    a_ab = _dot(a_s, bt_t, NT) * slm
    a_ak = _dot(a_s, kt_t, NT) * slm
    a_rb = _dot(r_s, bt_t, NT) * lm
    a_rk = _dot(r_s, kt_t, NT) * lm
    yield

    d = a_ab * dgm
    e = a_ab - d
    n_blk = d.shape[0] // SUB
    fold = lambda m: functools.reduce(
        lambda acc, b: acc + m[b * SUB:(b + 1) * SUB], range(1, n_blk), m[0:SUB])
    unfold = lambda strip: jnp.concatenate([strip] * n_blk, axis=0) * bkm
    p_full = d
    p_strip = fold(d)
    d_inv_strip = fold(lm - slm) + p_strip
    n = 2
    while n < SUB:
        p_strip = _dot(p_strip, p_full)
        yield
        p_full = unfold(p_strip)
        d_inv_strip = d_inv_strip + _dot(d_inv_strip, p_full)
        yield
        n *= 2
    d_inv = unfold(d_inv_strip)
    f = _dot(d_inv, e)
    yield
    t_inv = d_inv
    fp = f
    n = 1
    terms = []
    while n < L // SUB:
        terms.append(fp)
        n *= 2
        if n < L // SUB:
            fp = _dot(fp, fp)
            yield
    for fp in reversed(terms):
        t_inv = t_inv + _dot(fp, t_inv)
        yield

    while state_version.get(idx, 0) != version:
        yield
    s0 = s_ref[idx]
    s_mid = s0 * g_mid
    rhs = _dot(a_s, s_mid, NT) + _dot(a_ak, v_s)
    y_part = _dot(r_s, s_mid, NT) + _dot(a_rk, v_s)
    yield
    u = _dot(t_inv, rhs)
    yield
    y_s = y_part + _dot(a_rb, u)
    s_ref[idx] = s0 * g_end + _dot(u, bh_s, TN) + _dot(v_s, kh_s, TN)
    state_version[idx] = version + 1
    y = y_s[0:L]
    for q in range(1, QUAD):
        y = y + y_s[q * L:(q + 1) * L]
    store_y(y)


def _dot_hl3(tri, lw):
    hi = lw.astype(BF16)
    r1 = lw - hi.astype(F32)
    mid = r1.astype(BF16)
    lo = (r1 - mid.astype(F32)).astype(BF16)
    d = functools.partial(jnp.dot, preferred_element_type=F32)
    return d(tri, hi) + (d(tri, mid) + d(tri, lo))


def _rwkv(r, lw, k, v, kk, b):
    B, S, A = r.shape
    W = QUAD * HEAD
    L = CHUNK
    idx = jnp.arange(W)
    assert L == HEAD, "stacked rows (head, time) and lanes (head, channel) share one block size"
    same = (idx[:, None] // L) == (idx[None, :] // L)
    ti = idx[:, None] % L
    tj = idx[None, :] % L
    bdm = same.astype(F32)
    slm = (same & (ti > tj)).astype(F32)
    lm = (same & (ti >= tj)).astype(F32)
    bkm = (same & (ti // SUB == tj // SUB)).astype(F32)
    dgm = (same & (ti > tj) & (ti // SUB == tj // SUB)).astype(F32)
    tri = (jnp.arange(L)[:, None] >= jnp.arange(L)[None, :]).astype(BF16)
    rows = RWKV_STEP_CHUNKS * L
    tok = pl.BlockSpec((B, rows, A), lambda c: (0, c, 0))
    full = lambda a: pl.BlockSpec(a.shape, lambda c: (0,) * a.ndim)
    return pl.pallas_call(
        _rwkv_kernel,
        grid=(S // rows,),
        in_specs=[tok] * 6 + [full(a) for a in (tri, bdm, slm, lm, dgm, bkm)],
        out_specs=tok,
        out_shape=jax.ShapeDtypeStruct((B, S, A), F32),
        scratch_shapes=[pltpu.VMEM((B * (A // W), W, W), F32)],
        compiler_params=_cparams(1),
        name="rwkv7",
    )(r, lw, k, v, kk, b, tri, bdm, slm, lm, dgm, bkm)


def _wide_block_dot(a, w_ref):
    w = w_ref.shape[-1]
    parts = [_dot(a[:, i * w:(i + 1) * w], w_ref[i]) for i in range(w_ref.shape[0])]
    return parts[0] if len(parts) == 1 else jnp.concatenate(parts, axis=1)


def _rglru_tile(xb, gb, cw_ref, cb_ref, wr_ref, br_ref, wi_ref, bi_ref, lam_ref,
                xc_carry, h_carry, y_out):
    tt, width = xb.shape
    ext = jnp.concatenate([xc_carry[...], xb], axis=0)
    xc_carry[...] = xb[tt - 8:tt, :]
    xc = cb_ref[...] + cw_ref[CONV_WIDTH - 1:CONV_WIDTH, :] * xb
    for d in range(1, CONV_WIDTH):
        xc = xc + cw_ref[CONV_WIDTH - 1 - d:CONV_WIDTH - d, :] * ext[8 - d:8 - d + tt, :]

    pre_r = _wide_block_dot(xc, wr_ref)
    pre_i = _wide_block_dot(xc, wi_ref)
    yield
    gate_r = _sigmoid(pre_r + br_ref[...])
    gate_i = _sigmoid(pre_i + bi_ref[...])
    log_a = -LRU_C * gate_r * _softplus(-lam_ref[...])
    a = jnp.exp(log_a)
    u = jnp.sqrt(-jnp.tanh(log_a) * (a * a + 1.0)) * (gate_i * xc)
    yield

    a3 = a.reshape(tt // 8, 8, width)
    u3 = u.reshape(tt // 8, 8, width)
    sub = lax.broadcasted_iota(jnp.int32, (1, 8, 1), 1)
    d = 1
    while d < 8:
        keep = sub >= d
        a_sh = jnp.where(keep, pltpu.roll(a3, d, axis=1), 1.0)
        u_sh = jnp.where(keep, pltpu.roll(u3, d, axis=1), 0.0)
        u3 = a3 * u_sh + u3
        a3 = a3 * a_sh
        d *= 2
        yield
    gate = _gelu_tanh(gb)
    carry = h_carry[0:1, :]
    groups = []
    for i in range(tt // 8):
        hg = a3[i] * carry + u3[i]
        groups.append(hg)
        carry = hg[7:8, :]
        if i % 8 == 7:
            yield
    h_carry[0:1, :] = carry
    y_out[0] = jnp.concatenate(groups, axis=0) * gate


def _merge_kernel(x_ref, ya_ref, bonus_ref, g_ref, yb_ref, s1_ref, s2_ref, gt1_ref, sh2_ref,
                  sc2_ref, lng_ref, lnb_ref, bd_ref, pa_ref, pb_ref, wout_ref, gffn_ref,
                  wr_ref, br_ref, x1_out, h2_out, logit_out):
    ya = ya_ref[0]
    bd = bd_ref[...]
    inv_n = 1.0 / HEAD
    mean = _head_sums(ya, bd) * inv_n
    yc = ya - mean
    var = _head_sums(yc * yc, bd) * inv_n
    yn = yc * lax.rsqrt(var + GN_EPS) * lng_ref[...] + lnb_ref[...]
    ya2 = (yn + bonus_ref[0]) * g_ref[0]

    merged = s1_ref[0] * _dot(ya2, pa_ref[...]) + s2_ref[0] * _dot(yb_ref[0], pb_ref[...])
    x1 = x_ref[0] + gt1_ref[0] * _dot(merged, wout_ref[...])
    x1_out[0] = x1

    ms = jnp.mean(x1 * x1, axis=-1, keepdims=True)
    h2 = (x1 * lax.rsqrt(ms + RMS_EPS)) * gffn_ref[...]
    h2 = h2 * (1.0 + sc2_ref[0]) + sh2_ref[0]
    h2_out[0] = h2.astype(BF16)

    n_log = br_ref.shape[1]
    h2_hi, h2_lo = _split(h2)
    parts = jnp.dot(jnp.concatenate([h2_hi, h2_lo], axis=0), wr_ref[...],
                    preferred_element_type=F32)
    tm = h2.shape[0]
    logit_out[0] = ((parts[:tm, :n_log] + parts[:tm, n_log:])
                    + (parts[tm:, :n_log] + parts[tm:, n_log:])) + br_ref[...]


def _router_kernel(logit_ref, tri_ref, utri_ref, meta_out, metat_out, seg_out):
    logits = logit_ref[0]
    lane = lax.broadcasted_iota(jnp.int32, logits.shape, 1).astype(F32)
    neg = jnp.float32(-jnp.inf)
    big = jnp.float32(1e9)
    is_g = lane < N_GROUPS
    lg = jnp.where(is_g, logits, neg)
    mg = jnp.max(lg, axis=-1, keepdims=True)
    pg_top = 1.0 / jnp.sum(jnp.exp(lg - mg), axis=-1, keepdims=True)
    g_idx = jnp.min(jnp.where(lg == mg, lane, big), axis=-1, keepdims=True)
    e_lane = lane - N_GROUPS
    in_grp = (e_lane >= g_idx * EXPERTS_PER_GROUP) & (e_lane < (g_idx + 1) * EXPERTS_PER_GROUP)
    le = jnp.where(in_grp, logits, neg)
    me = jnp.max(le, axis=-1, keepdims=True)
    i1 = jnp.min(jnp.where(le == me, lane, big), axis=-1, keepdims=True)
    ee = jnp.exp(le - me)
    se = jnp.sum(ee, axis=-1, keepdims=True)
    rest = jnp.where(lane != i1, le, neg)
    m2 = jnp.max(rest, axis=-1, keepdims=True)
    i2 = jnp.min(jnp.where(rest == m2, lane, big), axis=-1, keepdims=True)
    p1 = 1.0 / se
    p2 = jnp.exp(m2 - me) / se
    den = p1 + p2
    wt1 = pg_top * (p1 / den)
    wt2 = pg_top * (p2 / den)

    hot1 = jnp.where(lane == i1, 1.0, 0.0)
    hot2 = jnp.where(lane == i2, 1.0, 0.0)
    both = hot1 + hot2
    n_sub = logits.shape[0] // MOE_SUB
    pos1, pos2, seg_rows = [], [], []
    for s in range(n_sub):
        rows = slice(s * MOE_SUB, (s + 1) * MOE_SUB)
        before = _dot(tri_ref[...], both[rows])
        count = jnp.sum(both[rows], axis=0, keepdims=True)
        padded = jnp.floor((count + (MOE_ALIGN - 1)) * (1.0 / MOE_ALIGN)) * MOE_ALIGN
        start = _dot_hl(jnp.broadcast_to(padded, (8, padded.shape[1])), utri_ref[...])[0:1] \
            + float(s * MOE_CAP)
        slot = before + start
        pos1.append(jnp.sum(hot1[rows] * slot, axis=-1, keepdims=True))
        pos2.append(jnp.sum(hot2[rows] * slot, axis=-1, keepdims=True))
        seg_rows.append((start, count))
    pos1 = jnp.concatenate(pos1, axis=0)
    pos2 = jnp.concatenate(pos2, axis=0)
    meta = jnp.where(lane == 0, pos1, jnp.where(lane == 1, pos2,
                     jnp.where(lane == 2, wt1, jnp.where(lane == 3, wt2, 0.0))))
    meta_out[0] = meta
    metat_out[0] = jnp.transpose(meta)[0:8, :]
    seg = [st for st, _ in seg_rows] + [ct for _, ct in seg_rows]
    seg_out[0, 0] = jnp.concatenate(seg, axis=0)


def _router(logits, *, tm):
    B, S, n = logits.shape
    assert 2 * (tm // MOE_SUB) == 8, "segment table holds one start and one count row per sub-tile"
    tri = (jnp.arange(MOE_SUB)[:, None] > jnp.arange(MOE_SUB)[None, :]).astype(BF16)
    utri = (jnp.arange(n)[:, None] < jnp.arange(n)[None, :]).astype(BF16)
    tok = pl.BlockSpec((1, tm, n), lambda b, s: (b, s, 0))
    full = lambda a: pl.BlockSpec(a.shape, lambda b, s: (0,) * a.ndim)
    return pl.pallas_call(
        _router_kernel,
        grid=(B, S // tm),
        in_specs=[tok, full(tri), full(utri)],
        out_specs=[tok, pl.BlockSpec((1, 8, tm), lambda b, s: (b, 0, s)),
                   pl.BlockSpec((1, 1, 8, n), lambda b, s: (b, s, 0, 0))],
        out_shape=[jax.ShapeDtypeStruct((B, S, n), F32),
                   jax.ShapeDtypeStruct((B, 8, S), F32),
                   jax.ShapeDtypeStruct((B, S // tm, 8, n), F32)],
        compiler_params=_cparams(2),
        name="router",
    )(logits, tri, utri)


def _merge(x, ya, bonus, g, yb, s1, s2, gt1, sh2, sc2, lnx_g, lnx_b, bd, pa, pb, wout, g_ffn,
           wr, br, *, tm=512):
    B, S, D = x.shape
    A = ya.shape[-1]
    n = br.shape[-1]
    row = lambda a: a.reshape(1, -1)
    full = lambda a: pl.BlockSpec(a.shape, lambda b, s: (0,) * a.ndim)
    tok = lambda n: pl.BlockSpec((1, tm, n), lambda b, s: (b, s, 0))
    per_b = pl.BlockSpec((1, 1, D), lambda b, s: (b, 0, 0))
    small = [row(lnx_g), row(lnx_b), bd, pa, pb, wout, row(g_ffn), wr, row(br)]
    return pl.pallas_call(
        _merge_kernel,
        grid=(B, S // tm),
        in_specs=[tok(D), tok(A), tok(A), tok(A), tok(A), tok(D), tok(D), per_b, per_b, per_b]
        + [full(a) for a in small],
        out_specs=[tok(D), tok(D), tok(n)],
        out_shape=[jax.ShapeDtypeStruct((B, S, D), F32),
                   jax.ShapeDtypeStruct((B, S, D), BF16),
                   jax.ShapeDtypeStruct((B, S, n), F32)],
        compiler_params=_cparams(2),
        name="merge",
    )(x, ya, bonus, g, yb, s1, s2, gt1.reshape(B, 1, D), sh2.reshape(B, 1, D),
      sc2.reshape(B, 1, D), *small)


def _moe_kernel(seg, x1_ref, h2_ref, meta_ref, metat_ref, w1_ref, w3_ref, w2_ref,
                gt2_ref, gf_ref, shf_ref, scf_ref, o_ref, buf_ref, hid_ref):
    b = pl.program_id(0)
    i = pl.program_id(1)
    g = pl.program_id(2)
    n_g = pl.num_programs(2)
    tm = h2_ref.shape[1]
    n_sub = tm // MOE_SUB
    W = MOE_WINDOW
    base = (b * pl.num_programs(1) + i) * (2 * n_sub * N_EXPERTS)

    def local(slots, s):
        return slots[:, s * MOE_SUB:(s + 1) * MOE_SUB] - float(s * MOE_CAP)

    pos1_row = metat_ref[0, 0:1, :]
    pos2_row = metat_ref[0, 1:2, :]
    wt1_row = metat_ref[0, 2:3, :]
    wt2_row = metat_ref[0, 3:4, :]

    @pl.when(g == 0)
    def _():
        blk = MXU_TILE
        for s in range(n_sub):
            p1 = local(pos1_row, s)
            p2 = local(pos2_row, s)
            h2s = h2_ref[0, s * MOE_SUB:(s + 1) * MOE_SUB, :]
            for r0 in range(0, MOE_CAP, blk):
                rid = lax.broadcasted_iota(jnp.int32, (blk, 1), 0).astype(F32) + float(r0)
                hot = jnp.where((p1 == rid) | (p2 == rid), 1.0, 0.0).astype(BF16)
                buf_ref[s * MOE_CAP + r0:s * MOE_CAP + r0 + blk, :] = jnp.dot(
                    hot, h2s, preferred_element_type=F32).astype(BF16)

    wrow = lax.broadcasted_iota(jnp.int32, (W, 1), 0)
    ge0 = g * EXPERTS_PER_GROUP
    first = [jnp.int32(0)]
    for e in range(EXPERTS_PER_GROUP):
        n_max = functools.reduce(
            jnp.maximum, [seg[base + (n_sub + s) * N_EXPERTS + ge0 + e] for s in range(n_sub)])
        first.append(first[-1] + (n_max + W - 1) // W)
    total = first[-1]
    n_f = w1_ref.shape[-1]

    def locate(k):
        inside = jnp.logical_and(k >= 0, k < total)
        e = functools.reduce(lambda a, c: a + (k >= c).astype(jnp.int32),
                             first[1:EXPERTS_PER_GROUP], jnp.int32(0))
        w = k - functools.reduce(lambda a, ec: jnp.where(e == ec[0], ec[1], a),
                                 enumerate(first[:-1]), jnp.int32(0))
        e = jnp.where(inside, e, EXPERTS_PER_GROUP - 1)
        w = jnp.where(inside, w, MOE_CAP // W)
        starts = [seg[base + s * N_EXPERTS + ge0 + e] for s in range(n_sub)]
        counts = [seg[base + (n_sub + s) * N_EXPERTS + ge0 + e] for s in range(n_sub)]
        offs = [pl.multiple_of(jnp.minimum(starts[s] + w * W, (s + 1) * MOE_CAP - W), MOE_ALIGN)
                for s in range(n_sub)]
        return e, w, offs, starts, counts

    def read_windows(it):
        out = []
        for j in range(MOE_ITEMS):
            e, w, offs, starts, counts = locate(it * MOE_ITEMS + j)
            out.append((e, jnp.concatenate(
                [buf_ref[pl.ds(offs[s], W), :] for s in range(n_sub)], axis=0)))
        return out

    def project_up(windows):
        for j, (e, xcat) in enumerate(windows):
            hid_ref[j, :, 0:n_f] = jnp.dot(xcat, w1_ref[e], preferred_element_type=F32)
            hid_ref[j, :, n_f:2 * n_f] = jnp.dot(xcat, w3_ref[e], preferred_element_type=F32)

    def finish(it, hidden):
        for j in range(MOE_ITEMS):
            e, w, offs, starts, counts = locate(it * MOE_ITEMS + j)
            hid = hidden[j]
            he = _silu(hid[:, 0:n_f]) * hid[:, n_f:2 * n_f]
            y = _dot(he, w2_ref[e])
            for s in range(n_sub):
                rid = (starts[s] + w * W + wrow).astype(F32)
                cols = slice(s * MOE_SUB, (s + 1) * MOE_SUB)
                wt = jnp.sum(jnp.where(pos1_row[:, cols] == rid, wt1_row[:, cols], 0.0)
                             + jnp.where(pos2_row[:, cols] == rid, wt2_row[:, cols], 0.0),
                             axis=-1, keepdims=True)
                valid = wrow < counts[s] - w * W
                new = (wt * y[s * W:(s + 1) * W]).astype(BF16)
                old = buf_ref[pl.ds(offs[s], W), :]
                buf_ref[pl.ds(offs[s], W), :] = jnp.where(valid, new, old)

    def pipeline(it, carry):
        windows = read_windows(it)
        hidden = [hid_ref[j] for j in range(MOE_ITEMS)]
        project_up(windows)
        finish(it - 1, hidden)
        return carry

    n_iter = (total + MOE_ITEMS - 1) // MOE_ITEMS

    @pl.when(total > 0)
    def _():
        project_up(read_windows(0))

    lax.fori_loop(1, n_iter, pipeline, 0)

    @pl.when(total > 0)
    def _():
        finish(n_iter - 1, [hid_ref[j] for j in range(MOE_ITEMS)])

    @pl.when(g == n_g - 1)
    def _():
        meta = meta_ref[0]
        slot_ids = lax.broadcasted_iota(jnp.int32, (1, MOE_CAP), 1).astype(F32)
        for s in range(n_sub):
            rows = slice(s * MOE_SUB, (s + 1) * MOE_SUB)
            p1 = meta[rows, 0:1] - float(s * MOE_CAP)
            p2 = meta[rows, 1:2] - float(s * MOE_CAP)
            hot = jnp.where((p1 == slot_ids) | (p2 == slot_ids), 1.0, 0.0).astype(BF16)
            moe = jnp.dot(hot, buf_ref[s * MOE_CAP:(s + 1) * MOE_CAP, :],
                          preferred_element_type=F32)
            x2 = x1_ref[0, rows, :] + gt2_ref[0] * moe
            ms = jnp.mean(x2 * x2, axis=-1, keepdims=True)
            y = (x2 * lax.rsqrt(ms + RMS_EPS)) * gf_ref[...]
            o_ref[0, rows, :] = y * (1.0 + scf_ref[0]) + shf_ref[0]


def _moe(seg, x1, h2, meta, metat, w1, w3, w2, gt2, g_final, shf, scf, *, tm):
    B, S, D = x1.shape
    E, _, F = w1.shape
    G = EXPERTS_PER_GROUP
    assert E == N_EXPERTS and MOE_CAP % MXU_TILE == 0 and tm % MOE_SUB == 0
    tok = lambda n: pl.BlockSpec((1, tm, n), lambda b, s, g, c: (b, s, 0))
    per_b = pl.BlockSpec((1, 1, D), lambda b, s, g, c: (b, 0, 0))
    grid_spec = pltpu.PrefetchScalarGridSpec(
        num_scalar_prefetch=1,
        grid=(B, S // tm, E // G),
        in_specs=[tok(D), tok(D), tok(128),
                  pl.BlockSpec((1, 8, tm), lambda b, s, g, c: (b, 0, s)),
                  pl.BlockSpec((G, D, F), lambda b, s, g, c: (g, 0, 0)),
                  pl.BlockSpec((G, D, F), lambda b, s, g, c: (g, 0, 0)),
                  pl.BlockSpec((G, F, D), lambda b, s, g, c: (g, 0, 0)),
                  per_b, pl.BlockSpec((1, D), lambda b, s, g, c: (0, 0)), per_b, per_b],
        out_specs=tok(D),
        scratch_shapes=[pltpu.VMEM(((tm // MOE_SUB) * MOE_CAP, D), BF16),
                        pltpu.VMEM((MOE_ITEMS, (tm // MOE_SUB) * MOE_WINDOW, 2 * F), F32)])
    return pl.pallas_call(
        _moe_kernel,
        grid_spec=grid_spec,
        out_shape=jax.ShapeDtypeStruct((B, S, D), F32),
        compiler_params=_cparams(3),
        name="moe",
    )(seg, x1, h2, meta, metat, w1, w3, w2, gt2.reshape(B, 1, D), g_final.reshape(1, D),
      shf.reshape(B, 1, D), scf.reshape(B, 1, D))


def _block_diag(w):
    G, n, _ = w.shape
    eye = jnp.eye(G, dtype=w.dtype)
    return (eye[:, None, :, None] * w[:, :, None, :]).reshape(G * n, G * n)


def kernel(x, c, w_ada, b_ada, g_mix, w_in, mu_rkv, mu_wag, w0, w1, w2, a0, a1, a2, g1, g2, k_k, k_a, r_k, lnx_g, lnx_b, conv_w, conv_b, w_rgate, b_rgate, w_igate, b_igate, lam, p_a, p_b, w_out, g_ffn, w_rg, b_rg, w_re, b_re, w1e, w3e, w2e, g_final, w_ada_f, b_ada_f):
    B, S, D = x.shape
    depth = w_ada.shape[0]
    assert depth == 1, "the final norm is fused into the last MoE call; one layer supported"
    A = w0.shape[-1]
    Bw = lam.shape[-1]

    c8 = jnp.zeros((8, D), F32).at[:B].set(c)
    modf = _ada(c8, w_ada_f, b_ada_f)[:B]
    shf, scf = modf[:, :D], modf[:, D:]

    head_ids = jnp.arange(QUAD * HEAD) // HEAD
    bd = (head_ids[:, None] == head_ids[None, :]).astype(BF16)

    for l in range(depth):
        mod = _ada(c8, w_ada[l], b_ada[l])[:B]
        sh1, sc1, gt1, sh2, sc2, gt2 = (mod[:, i * D:(i + 1) * D] for i in range(6))

        lowrank = jnp.concatenate([w1[l], a1[l], g1[l]], axis=1)
        mu_cols = jnp.concatenate([jnp.broadcast_to(mu_wag[l, 0][:, None], w1[l].shape),
                                   jnp.broadcast_to(mu_wag[l, 1][:, None], a1[l].shape),
                                   jnp.broadcast_to(mu_wag[l, 2][:, None], g1[l].shape)], axis=1)
        o1 = 3 * A
        wext = jnp.concatenate([w_in[l][:, :o1], mu_cols * lowrank, (1.0 - mu_cols) * lowrank,
                                w_in[l][:, o1:]], axis=1).astype(BF16)
        ra, rb = w2.shape[1], g2.shape[1]
        w2cat = jnp.zeros((2 * ra + rb, 3 * A), F32)
        w2cat = w2cat.at[0:ra, 0:A].set(w2[l]).at[ra:2 * ra, A:2 * A].set(a2[l])
        w2cat = w2cat.at[2 * ra:, 2 * A:].set(g2[l]).astype(BF16)

        per_tile = QUAD * HEAD // w_rgate.shape[-1]
        wide = lambda w: jax.vmap(_block_diag)(
            w.reshape(-1, per_tile, *w.shape[1:])).astype(BF16)
        (r, lw, k, v, kk, bvec, g, bonus, yb, s1, s2) = _inproj(
            x, sh1, sc1, g_mix[l], wext, mu_rkv[l], w0[l], a0[l], w2cat, k_k[l], k_a[l],
            r_k[l].reshape(-1), bd, conv_w[l].reshape(CONV_WIDTH, Bw), conv_b[l],
            wide(w_rgate[l]), b_rgate[l], wide(w_igate[l]), b_igate[l], lam[l])

        ya = _rwkv(r, lw, k, v, kk, bvec)

        n_g, n_e = w_rg.shape[-1], w_re.shape[-1]
        wr = jnp.zeros((D, 128), F32).at[:, :n_g].set(w_rg[l]).at[:, n_g:n_g + n_e].set(w_re[l])
        br = jnp.zeros((128,), F32).at[:n_g].set(b_rg[l]).at[n_g:n_g + n_e].set(b_re[l])
        wr = jnp.concatenate(_split(wr), axis=1)
        x1, h2, logits = _merge(
            x, ya, bonus, g, yb, s1, s2, gt1, sh2, sc2, lnx_g[l], lnx_b[l], bd,
            p_a[l].astype(BF16), p_b[l].astype(BF16), w_out[l].astype(BF16),
            g_ffn[l], wr, br)
        meta, metat, seg = _router(logits, tm=MOE_TILE)
        seg = seg[:, :, :, n_g:n_g + n_e].astype(jnp.int32).reshape(-1)
        x = _moe(seg, x1, h2, meta, metat, w1e[l].astype(BF16), w3e[l].astype(BF16),
                 w2e[l].astype(BF16), gt2, g_final, shf, scf, tm=MOE_TILE)
    return x
```

```python
import functools

import jax
import jax.numpy as jnp
from jax import lax
from jax.experimental import pallas as pl
from jax.experimental.pallas import tpu as pltpu

F32 = jnp.float32
BF16 = jnp.bfloat16

MXU_TILE = 256
HEAD = 64
CHUNK = 64
SUB = 16
QUAD = 4
RWKV_STEP_CHUNKS = 4
RMS_EPS = 1e-6
GN_EPS = 64e-5
LRU_C = 8.0
CONV_WIDTH = 4
N_GROUPS = 4
EXPERTS_PER_GROUP = 8
N_EXPERTS = N_GROUPS * EXPERTS_PER_GROUP
MOE_TILE = 1024
MOE_SUB = 256
MOE_ALIGN = 16
MOE_WINDOW = 32
MOE_ITEMS = 4
MOE_CAP = 2 * MOE_SUB + N_EXPERTS * (MOE_ALIGN - 1) + MOE_WINDOW
VMEM_LIMIT = 58 * 1024 * 1024


def _cparams(n_axes):
    return pltpu.CompilerParams(dimension_semantics=("arbitrary",) * n_axes,
                                vmem_limit_bytes=VMEM_LIMIT)


def _dot(a, b, dims=(((1,), (0,)), ((), ()))):
    return lax.dot_general(a.astype(BF16), b.astype(BF16), dims,
                           preferred_element_type=F32)


NT = (((1,), (1,)), ((), ()))
TN = (((0,), (0,)), ((), ()))


def _split(a):
    hi = a.astype(BF16)
    lo = (a - hi.astype(F32)).astype(BF16)
    return hi, lo


def _dot_hl(a, b_exact, dims=(((1,), (0,)), ((), ()))):
    hi, lo = _split(a)
    return (lax.dot_general(hi, b_exact, dims, preferred_element_type=F32)
            + lax.dot_general(lo, b_exact, dims, preferred_element_type=F32))


def _head_sums(a, ones_blocks):
    w = ones_blocks.shape[0]
    parts = [_dot_hl(a[:, o:o + w], ones_blocks) for o in range(0, a.shape[1], w)]
    return parts[0] if len(parts) == 1 else jnp.concatenate(parts, axis=1)


def _dot3(a, b, dims=(((1,), (0,)), ((), ()))):
    ah, al = _split(a)
    bh, bl = _split(b)
    d = functools.partial(lax.dot_general, dimension_numbers=dims,
                          preferred_element_type=F32)
    return d(ah, bh) + (d(ah, bl) + d(al, bh))


def _sigmoid(z):
    return 0.5 * jnp.tanh(0.5 * z) + 0.5


def _softplus(z):
    return jnp.maximum(z, 0.0) + jnp.log1p(jnp.exp(-jnp.abs(z)))


def _silu(z):
    return z * _sigmoid(z)


def _gelu_tanh(z):
    return 0.5 * z * (1.0 + jnp.tanh(0.7978845608028654 * (z + 0.044715 * (z * z * z))))


def _ada_kernel(c_ref, w_ref, b_ref, o_ref):
    ca = _silu(c_ref[...])
    o_ref[...] = _dot3(ca, w_ref[...]) + b_ref[...]


def _ada(c8, w, b, tn=1024):
    d, n = w.shape
    return pl.pallas_call(
        _ada_kernel,
        grid=(n // tn,),
        in_specs=[pl.BlockSpec((8, d), lambda j: (0, 0)),
                  pl.BlockSpec((d, tn), lambda j: (0, j)),
                  pl.BlockSpec((1, tn), lambda j: (0, j))],
        out_specs=pl.BlockSpec((8, tn), lambda j: (0, j)),
        out_shape=jax.ShapeDtypeStruct((8, n), F32),
        compiler_params=_cparams(1),
        name="adaln",
    )(c8, w, b.reshape(1, n))


def _inproj_kernel(x_ref, sh_ref, sc_ref, g_ref, wext_ref, mu_ref, w0_ref, a0_ref,
                   w2cat_ref, kk_ref, ka_ref, rk_ref, bd_ref,
                   cw_ref, cb_ref, wr_ref, br_ref, wi_ref, bi_ref, lam_ref,
                   r_out, lw_out, k_out, v_out, kk_out, b_out, g_out, bonus_out,
                   yb_out, s1_out, s2_out, carry_ref, xc_carry, h_carry, *, a_width, b_width,
                   d_model):
    A = a_width
    n_shift = 3 * A + 256
    s = pl.program_id(1)

    @pl.when(s == 0)
    def _():
        carry_ref[...] = jnp.zeros_like(carry_ref)
        xc_carry[...] = jnp.zeros_like(xc_carry)
        h_carry[...] = jnp.zeros_like(h_carry)

    x = x_ref[0]
    tm = x.shape[0]
    ms = jnp.mean(x * x, axis=-1, keepdims=True)
    h = (x * lax.rsqrt(ms + RMS_EPS)) * g_ref[...]
    h = h * (1.0 + sc_ref[0]) + sh_ref[0]
    hb = h.astype(BF16)

    sg = jnp.dot(hb, wext_ref[:, 0:n_shift + 256], preferred_element_type=F32)
    o = n_shift + 256
    xg = jnp.dot(hb, wext_ref[:, o:o + 2 * b_width], preferred_element_type=F32)
    o += 2 * b_width

    def gate_strand():
        step = 2 * MXU_TILE
        for out, base in ((s1_out, o), (s2_out, o + d_model)):
            for c in range(0, d_model, step):
                out[0, :, c:c + step] = _sigmoid(jnp.dot(
                    hb, wext_ref[:, base + c:base + c + step],
                    preferred_element_type=F32)).astype(out.dtype)
                yield

    def rwkv_strand():
        yield from _rwkv_heads(sg, n_shift, A, carry_ref, mu_ref, w0_ref, a0_ref, w2cat_ref,
                               kk_ref, ka_ref, rk_ref, bd_ref, r_out, lw_out, k_out, v_out,
                               kk_out, b_out, g_out, bonus_out)

    strands = [gate_strand(), rwkv_strand(),
               _rglru_tile(xg[:, 0:b_width], xg[:, b_width:2 * b_width], cw_ref, cb_ref, wr_ref,
                           br_ref, wi_ref, bi_ref, lam_ref, xc_carry, h_carry, yb_out)]
    while strands:
        strands = [c for c in strands if next(c, "done") != "done"]


def _rwkv_heads(sg, n_shift, A, carry_ref, mu_ref, w0_ref, a0_ref, w2cat_ref, kk_ref, ka_ref,
                rk_ref, bd_ref, r_out, lw_out, k_out, v_out, kk_out, b_out, g_out, bonus_out):
    tm = sg.shape[0]
    cur = sg[:, 0:n_shift]
    row = lax.broadcasted_iota(jnp.int32, (tm, 1), 0)
    prev = jnp.where(row == 0, carry_ref[0:1, :], pltpu.roll(cur, 1, axis=0))
    carry_ref[0:1, :] = cur[tm - 1:tm, :]

    rkv = cur[:, 0:3 * A]
    rkv = rkv + (prev[:, 0:3 * A] - rkv) * mu_ref[...]
    r = rkv[:, 0:A]
    k = rkv[:, A:2 * A]
    v = rkv[:, 2 * A:3 * A]
    r_out[0] = r
    v_out[0] = v
    yield

    pre = sg[:, n_shift:n_shift + 256] + prev[:, 3 * A:3 * A + 256]
    lane = lax.broadcasted_iota(jnp.int32, pre.shape, 1)
    act = jnp.where(lane < 64, jnp.tanh(pre), jnp.where(lane < 128, pre, _sigmoid(pre)))
    low = _dot(act, w2cat_ref[...])
    bd = bd_ref[...]
    kkr = k * kk_ref[...]
    kk_ss = _head_sums(kkr * kkr, bd)
    yield
    w_log = -_softplus(-(w0_ref[...] + low[:, 0:A])) - 0.5
    lw_out[0] = -jnp.exp(w_log)
    g_out[0] = low[:, 2 * A:3 * A]
    yield
    iclr = _sigmoid(a0_ref[...] + low[:, A:2 * A])
    kkn = kkr * lax.rsqrt(kk_ss + 1e-12)
    k2 = k * (1.0 + (iclr - 1.0) * ka_ref[...])
    k_out[0] = k2
    kk_out[0] = kkn
    b_out[0] = kkn * iclr
    yield
    bonus_out[0] = _head_sums(r * k2 * rk_ref[...], bd) * v


def _inproj(x, sh1, sc1, g_mix, wext, mu_rkv, w0, a0, w2cat, k_k, k_a, r_k, bd,
            conv_w, conv_b, wr, br, wi, bi, lam, *, tm=512):
    B, S, D = x.shape
    A = w0.shape[-1]
    Bw = lam.shape[-1]
    row = lambda a: a.reshape(1, -1)
    full = lambda a: pl.BlockSpec(a.shape, lambda b, s: (0,) * a.ndim,
                                  pipeline_mode=pl.Buffered(1))
    tok = lambda n: pl.BlockSpec((1, tm, n), lambda b, s: (b, s, 0))
    per_b = pl.BlockSpec((1, 1, D), lambda b, s: (b, 0, 0))
    small = [row(g_mix), wext, row(mu_rkv), row(w0), row(a0), w2cat, row(k_k), row(k_a),
             row(r_k), bd, conv_w, row(conv_b), wr, row(br), wi, row(bi), row(lam)]
    outs = [jax.ShapeDtypeStruct((B, S, A), F32)] * 8 + \
           [jax.ShapeDtypeStruct((B, S, Bw), BF16)] + \
           [jax.ShapeDtypeStruct((B, S, D), BF16)] * 2
    return pl.pallas_call(
        functools.partial(_inproj_kernel, a_width=A, b_width=Bw, d_model=D),
        grid=(B, S // tm),
        in_specs=[tok(D), per_b, per_b] + [full(a) for a in small],
        out_specs=[tok(A)] * 8 + [tok(Bw)] + [tok(D)] * 2,
        out_shape=outs,
        scratch_shapes=[pltpu.VMEM((8, 3 * A + 256), F32), pltpu.VMEM((8, Bw), F32),
                        pltpu.VMEM((8, Bw), F32)],
        compiler_params=_cparams(2),
        name="inproj",
    )(x, sh1.reshape(B, 1, D), sc1.reshape(B, 1, D), *small)


def _rwkv_kernel(r_ref, lw_ref, k_ref, v_ref, kk_ref, b_ref, tri_ref, bdm_ref, slm_ref,
                 lm_ref, dgm_ref, bkm_ref, y_ref, s_ref):
    @pl.when(pl.program_id(0) == 0)
    def _():
        s_ref[...] = jnp.zeros_like(s_ref)

    W = QUAD * HEAD
    n_b, rows, n_a = r_ref.shape
    masks = (tri_ref[...], bdm_ref[...], slm_ref[...], lm_ref[...], dgm_ref[...], bkm_ref[...])
    chains = []
    state_version = {}
    for c in range(rows // CHUNK):
        t = slice(c * CHUNK, (c + 1) * CHUNK)
        for bi in range(n_b):
            for q in range(n_a // W):
                cols = slice(q * W, (q + 1) * W)
                idx = bi * (n_a // W) + q
                ins = [ref[bi, t, cols] for ref in (r_ref, lw_ref, k_ref, v_ref, kk_ref, b_ref)]
                chains.append(_rwkv_chunk(*ins, s_ref, idx, masks, state_version, c,
                                          functools.partial(_store_y, y_ref, bi, t, cols)))
    while chains:
        chains = [c for c in chains if next(c, "done") != "done"]


def _store_y(y_ref, bi, t, cols, y):
    y_ref[bi, t, cols] = y


def _rwkv_chunk(r, lw, k, v, kk, b, s_ref, idx, masks, state_version, version, store_y):
    tri, bdm, slm, lm, dgm, bkm = masks
    L = CHUNK
    a = -kk

    cl = _dot_hl3(tri, lw)
    clp = cl - lw
    cm = cl[L // 2 - 1:L // 2, :]
    ce = cl[L - 1:L, :]
    at = a * jnp.exp(clp - cm)
    rt = r * jnp.exp(cl - cm)
    e_inv = jnp.exp(cm - cl)
    bt = b * e_inv
    kt = k * e_inv
    e_end = jnp.exp(ce - cl)
    bh = b * e_end
    kh = k * e_end
    g_mid = jnp.exp(cm)
    g_end = jnp.exp(ce)

    tile = lambda m: jnp.concatenate([m] * QUAD, axis=0)
    stack = lambda m: tile(m) * bdm
    a_s = stack(at)
    r_s = stack(rt)
    v_s = stack(v)
    bh_s = stack(bh)
    kh_s = stack(kh)
    bt_t = tile(bt)
    kt_t = tile(kt)
    yield

    a_ab = _dot(a_s, bt_t, NT) * slm
    a_ak = _dot(a_s, kt_t, NT) * slm
    a_rb = _dot(r_s, bt_t, NT) * lm
    a_rk = _dot(r_s, kt_t, NT) * lm
    yield

    d = a_ab * dgm
    e = a_ab - d
    n_blk = d.shape[0] // SUB
    fold = lambda m: functools.reduce(
        lambda acc, b: acc + m[b * SUB:(b + 1) * SUB], range(1, n_blk), m[0:SUB])
    unfold = lambda strip: jnp.concatenate([strip] * n_blk, axis=0) * bkm
    p_full = d
    p_strip = fold(d)
    d_inv_strip = fold(lm - slm) + p_strip
    n = 2
    while n < SUB:
        p_strip = _dot(p_strip, p_full)
        yield
        p_full = unfold(p_strip)
        d_inv_strip = d_inv_strip + _dot(d_inv_strip, p_full)
        yield
        n *= 2
    d_inv = unfold(d_inv_strip)
    f = _dot(d_inv, e)
    yield
    t_inv = d_inv
    fp = f
    n = 1
    terms = []
    while n < L // SUB:
        terms.append(fp)
        n *= 2
        if n < L // SUB:
            fp = _dot(fp, fp)
            yield
    for fp in reversed(terms):
        t_inv = t_inv + _dot(fp, t_inv)
        yield

    while state_version.get(idx, 0) != version:
        yield
    s0 = s_ref[idx]
    s_mid = s0 * g_mid
    rhs = _dot(a_s, s_mid, NT) + _dot(a_ak, v_s)
    y_part = _dot(r_s, s_mid, NT) + _dot(a_rk, v_s)
    yield
    u = _dot(t_inv, rhs)
    yield
    y_s = y_part + _dot(a_rb, u)
    s_ref[idx] = s0 * g_end + _dot(u, bh_s, TN) + _dot(v_s, kh_s, TN)
    state_version[idx] = version + 1
    y = y_s[0:L]
    for q in range(1, QUAD):
        y = y + y_s[q * L:(q + 1) * L]
    store_y(y)


def _dot_hl3(tri, lw):
    hi = lw.astype(BF16)
    r1 = lw - hi.astype(F32)
    mid = r1.astype(BF16)
    lo = (r1 - mid.astype(F32)).astype(BF16)
    d = functools.partial(jnp.dot, preferred_element_type=F32)
    return d(tri, hi) + (d(tri, mid) + d(tri, lo))


def _rwkv(r, lw, k, v, kk, b):
    B, S, A = r.shape
    W = QUAD * HEAD
    L = CHUNK
    idx = jnp.arange(W)
    assert L == HEAD, "stacked rows (head, time) and lanes (head, channel) share one block size"
    same = (idx[:, None] // L) == (idx[None, :] // L)
    ti = idx[:, None] % L
    tj = idx[None, :] % L
    bdm = same.astype(F32)
    slm = (same & (ti > tj)).astype(F32)
    lm = (same & (ti >= tj)).astype(F32)
    bkm = (same & (ti // SUB == tj // SUB)).astype(F32)
    dgm = (same & (ti > tj) & (ti // SUB == tj // SUB)).astype(F32)
    tri = (jnp.arange(L)[:, None] >= jnp.arange(L)[None, :]).astype(BF16)
    rows = RWKV_STEP_CHUNKS * L
    tok = pl.BlockSpec((B, rows, A), lambda c: (0, c, 0))
    full = lambda a: pl.BlockSpec(a.shape, lambda c: (0,) * a.ndim)
    return pl.pallas_call(
        _rwkv_kernel,
        grid=(S // rows,),
        in_specs=[tok] * 6 + [full(a) for a in (tri, bdm, slm, lm, dgm, bkm)],
        out_specs=tok,
        out_shape=jax.ShapeDtypeStruct((B, S, A), F32),
        scratch_shapes=[pltpu.VMEM((B * (A // W), W, W), F32)],
        compiler_params=_cparams(1),
        name="rwkv7",
    )(r, lw, k, v, kk, b, tri, bdm, slm, lm, dgm, bkm)


def _wide_block_dot(a, w_ref):
    w = w_ref.shape[-1]
    parts = [_dot(a[:, i * w:(i + 1) * w], w_ref[i]) for i in range(w_ref.shape[0])]
    return parts[0] if len(parts) == 1 else jnp.concatenate(parts, axis=1)


def _rglru_tile(xb, gb, cw_ref, cb_ref, wr_ref, br_ref, wi_ref, bi_ref, lam_ref,
                xc_carry, h_carry, y_out):
    tt, width = xb.shape
    ext = jnp.concatenate([xc_carry[...], xb], axis=0)
    xc_carry[...] = xb[tt - 8:tt, :]
    xc = cb_ref[...] + cw_ref[CONV_WIDTH - 1:CONV_WIDTH, :] * xb
    for d in range(1, CONV_WIDTH):
        xc = xc + cw_ref[CONV_WIDTH - 1 - d:CONV_WIDTH - d, :] * ext[8 - d:8 - d + tt, :]

    pre_r = _wide_block_dot(xc, wr_ref)
    pre_i = _wide_block_dot(xc, wi_ref)
    yield
    gate_r = _sigmoid(pre_r + br_ref[...])
    gate_i = _sigmoid(pre_i + bi_ref[...])
    log_a = -LRU_C * gate_r * _softplus(-lam_ref[...])
    a = jnp.exp(log_a)
    u = jnp.sqrt(-jnp.tanh(log_a) * (a * a + 1.0)) * (gate_i * xc)
    yield

    a3 = a.reshape(tt // 8, 8, width)
    u3 = u.reshape(tt // 8, 8, width)
    sub = lax.broadcasted_iota(jnp.int32, (1, 8, 1), 1)
    d = 1
    while d < 8:
        keep = sub >= d
        a_sh = jnp.where(keep, pltpu.roll(a3, d, axis=1), 1.0)
        u_sh = jnp.where(keep, pltpu.roll(u3, d, axis=1), 0.0)
        u3 = a3 * u_sh + u3
        a3 = a3 * a_sh
        d *= 2
        yield
    gate = _gelu_tanh(gb)
    carry = h_carry[0:1, :]
    groups = []
    for i in range(tt // 8):
        hg = a3[i] * carry + u3[i]
        groups.append(hg)
        carry = hg[7:8, :]
        if i % 8 == 7:
            yield
    h_carry[0:1, :] = carry
    y_out[0] = (jnp.concatenate(groups, axis=0) * gate).astype(y_out.dtype)


def _merge_kernel(x_ref, ya_ref, bonus_ref, g_ref, yb_ref, s1_ref, s2_ref, gt1_ref, sh2_ref,
                  sc2_ref, lng_ref, lnb_ref, bd_ref, pa_ref, pb_ref, wout_ref, gffn_ref,
                  wr_ref, br_ref, x1_out, h2_out, logit_out):
    ya = ya_ref[0]
    bd = bd_ref[...]
    inv_n = 1.0 / HEAD
    mean = _head_sums(ya, bd) * inv_n
    yc = ya - mean
    var = _head_sums(yc * yc, bd) * inv_n
    yn = yc * lax.rsqrt(var + GN_EPS) * lng_ref[...] + lnb_ref[...]
    ya2 = (yn + bonus_ref[0]) * g_ref[0]

    merged = s1_ref[0] * _dot(ya2, pa_ref[...]) + s2_ref[0] * _dot(yb_ref[0], pb_ref[...])
    x1 = x_ref[0] + gt1_ref[0] * _dot(merged, wout_ref[...])
    x1_out[0] = x1

    ms = jnp.mean(x1 * x1, axis=-1, keepdims=True)
    h2 = (x1 * lax.rsqrt(ms + RMS_EPS)) * gffn_ref[...]
    h2 = h2 * (1.0 + sc2_ref[0]) + sh2_ref[0]
    h2_out[0] = h2.astype(BF16)

    n_log = br_ref.shape[1]
    h2_hi, h2_lo = _split(h2)
    parts = jnp.dot(jnp.concatenate([h2_hi, h2_lo], axis=0), wr_ref[...],
                    preferred_element_type=F32)
    tm = h2.shape[0]
    logit_out[0] = ((parts[:tm, :n_log] + parts[:tm, n_log:])
                    + (parts[tm:, :n_log] + parts[tm:, n_log:])) + br_ref[...]


def _router_kernel(logit_ref, tri_ref, utri_ref, meta_out, metat_out, seg_out):
    logits = logit_ref[0]
    lane = lax.broadcasted_iota(jnp.int32, logits.shape, 1).astype(F32)
    neg = jnp.float32(-jnp.inf)
    big = jnp.float32(1e9)
    is_g = lane < N_GROUPS
    lg = jnp.where(is_g, logits, neg)
    mg = jnp.max(lg, axis=-1, keepdims=True)
    pg_top = 1.0 / jnp.sum(jnp.exp(lg - mg), axis=-1, keepdims=True)
    g_idx = jnp.min(jnp.where(lg == mg, lane, big), axis=-1, keepdims=True)
    e_lane = lane - N_GROUPS
    in_grp = (e_lane >= g_idx * EXPERTS_PER_GROUP) & (e_lane < (g_idx + 1) * EXPERTS_PER_GROUP)
    le = jnp.where(in_grp, logits, neg)
    me = jnp.max(le, axis=-1, keepdims=True)
    i1 = jnp.min(jnp.where(le == me, lane, big), axis=-1, keepdims=True)
    ee = jnp.exp(le - me)
    se = jnp.sum(ee, axis=-1, keepdims=True)
    rest = jnp.where(lane != i1, le, neg)
    m2 = jnp.max(rest, axis=-1, keepdims=True)
    i2 = jnp.min(jnp.where(rest == m2, lane, big), axis=-1, keepdims=True)
    p1 = 1.0 / se
    p2 = jnp.exp(m2 - me) / se
    den = p1 + p2
    wt1 = pg_top * (p1 / den)
    wt2 = pg_top * (p2 / den)

    hot1 = jnp.where(lane == i1, 1.0, 0.0)
    hot2 = jnp.where(lane == i2, 1.0, 0.0)
    both = hot1 + hot2
    n_sub = logits.shape[0] // MOE_SUB
    pos1, pos2, seg_rows = [], [], []
    for s in range(n_sub):
        rows = slice(s * MOE_SUB, (s + 1) * MOE_SUB)
        before = _dot(tri_ref[...], both[rows])
        count = jnp.sum(both[rows], axis=0, keepdims=True)
        padded = jnp.floor((count + (MOE_ALIGN - 1)) * (1.0 / MOE_ALIGN)) * MOE_ALIGN
        start = _dot_hl(jnp.broadcast_to(padded, (8, padded.shape[1])), utri_ref[...])[0:1] \
            + float(s * MOE_CAP)
        slot = before + start
        pos1.append(jnp.sum(hot1[rows] * slot, axis=-1, keepdims=True))
        pos2.append(jnp.sum(hot2[rows] * slot, axis=-1, keepdims=True))
        seg_rows.append((start, count))
    pos1 = jnp.concatenate(pos1, axis=0)
    pos2 = jnp.concatenate(pos2, axis=0)
    meta = jnp.where(lane == 0, pos1, jnp.where(lane == 1, pos2,
                     jnp.where(lane == 2, wt1, jnp.where(lane == 3, wt2, 0.0))))
    meta_out[0] = meta
    metat_out[0] = jnp.transpose(meta)[0:8, :]
    seg = [st for st, _ in seg_rows] + [ct for _, ct in seg_rows]
    seg_out[0, 0] = jnp.concatenate(seg, axis=0)


def _router(logits, *, tm):
    B, S, n = logits.shape
    assert 2 * (tm // MOE_SUB) == 8, "segment table holds one start and one count row per sub-tile"
    tri = (jnp.arange(MOE_SUB)[:, None] > jnp.arange(MOE_SUB)[None, :]).astype(BF16)
    utri = (jnp.arange(n)[:, None] < jnp.arange(n)[None, :]).astype(BF16)
    tok = pl.BlockSpec((1, tm, n), lambda b, s: (b, s, 0))
    full = lambda a: pl.BlockSpec(a.shape, lambda b, s: (0,) * a.ndim)
    return pl.pallas_call(
        _router_kernel,
        grid=(B, S // tm),
        in_specs=[tok, full(tri), full(utri)],
        out_specs=[tok, pl.BlockSpec((1, 8, tm), lambda b, s: (b, 0, s)),
                   pl.BlockSpec((1, 1, 8, n), lambda b, s: (b, s, 0, 0))],
        out_shape=[jax.ShapeDtypeStruct((B, S, n), F32),
                   jax.ShapeDtypeStruct((B, 8, S), F32),
                   jax.ShapeDtypeStruct((B, S // tm, 8, n), F32)],
        compiler_params=_cparams(2),
        name="router",
    )(logits, tri, utri)


def _merge(x, ya, bonus, g, yb, s1, s2, gt1, sh2, sc2, lnx_g, lnx_b, bd, pa, pb, wout, g_ffn,
           wr, br, *, tm=512):
    B, S, D = x.shape
    A = ya.shape[-1]
    n = br.shape[-1]
    row = lambda a: a.reshape(1, -1)
    full = lambda a: pl.BlockSpec(a.shape, lambda b, s: (0,) * a.ndim)
    tok = lambda n: pl.BlockSpec((1, tm, n), lambda b, s: (b, s, 0))
    per_b = pl.BlockSpec((1, 1, D), lambda b, s: (b, 0, 0))
    small = [row(lnx_g), row(lnx_b), bd, pa, pb, wout, row(g_ffn), wr, row(br)]
    return pl.pallas_call(
        _merge_kernel,
        grid=(B, S // tm),
        in_specs=[tok(D), tok(A), tok(A), tok(A), tok(A), tok(D), tok(D), per_b, per_b, per_b]
        + [full(a) for a in small],
        out_specs=[tok(D), tok(D), tok(n)],
        out_shape=[jax.ShapeDtypeStruct((B, S, D), F32),
                   jax.ShapeDtypeStruct((B, S, D), BF16),
                   jax.ShapeDtypeStruct((B, S, n), F32)],
        compiler_params=_cparams(2),
        name="merge",
    )(x, ya, bonus, g, yb, s1, s2, gt1.reshape(B, 1, D), sh2.reshape(B, 1, D),
      sc2.reshape(B, 1, D), *small)


def _moe_kernel(seg, x1_ref, h2_ref, meta_ref, metat_ref, w1_ref, w3_ref, w2_ref,
                gt2_ref, gf_ref, shf_ref, scf_ref, o_ref, buf_ref, hid_ref):
    b = pl.program_id(0)
    i = pl.program_id(1)
    g = pl.program_id(2)
    n_g = pl.num_programs(2)
    tm = h2_ref.shape[1]
    n_sub = tm // MOE_SUB
    W = MOE_WINDOW
    base = (b * pl.num_programs(1) + i) * (2 * n_sub * N_EXPERTS)

    def local(slots, s):
        return slots[:, s * MOE_SUB:(s + 1) * MOE_SUB] - float(s * MOE_CAP)

    pos1_row = metat_ref[0, 0:1, :]
    pos2_row = metat_ref[0, 1:2, :]
    wt1_row = metat_ref[0, 2:3, :]
    wt2_row = metat_ref[0, 3:4, :]

    @pl.when(g == 0)
    def _():
        blk = MXU_TILE
        for s in range(n_sub):
            p1 = local(pos1_row, s)
            p2 = local(pos2_row, s)
            h2s = h2_ref[0, s * MOE_SUB:(s + 1) * MOE_SUB, :]
            for r0 in range(0, MOE_CAP, blk):
                rid = lax.broadcasted_iota(jnp.int32, (blk, 1), 0).astype(F32) + float(r0)
                hot = jnp.where((p1 == rid) | (p2 == rid), 1.0, 0.0).astype(BF16)
                buf_ref[s * MOE_CAP + r0:s * MOE_CAP + r0 + blk, :] = jnp.dot(
                    hot, h2s, preferred_element_type=F32).astype(BF16)

    wrow = lax.broadcasted_iota(jnp.int32, (W, 1), 0)
    ge0 = g * EXPERTS_PER_GROUP
    first = [jnp.int32(0)]
    for e in range(EXPERTS_PER_GROUP):
        n_max = functools.reduce(
            jnp.maximum, [seg[base + (n_sub + s) * N_EXPERTS + ge0 + e] for s in range(n_sub)])
        first.append(first[-1] + (n_max + W - 1) // W)
    total = first[-1]
    n_f = w1_ref.shape[-1]

    def locate(k):
        inside = jnp.logical_and(k >= 0, k < total)
        e = functools.reduce(lambda a, c: a + (k >= c).astype(jnp.int32),
                             first[1:EXPERTS_PER_GROUP], jnp.int32(0))
        w = k - functools.reduce(lambda a, ec: jnp.where(e == ec[0], ec[1], a),
                                 enumerate(first[:-1]), jnp.int32(0))
        e = jnp.where(inside, e, EXPERTS_PER_GROUP - 1)
        w = jnp.where(inside, w, MOE_CAP // W)
        starts = [seg[base + s * N_EXPERTS + ge0 + e] for s in range(n_sub)]
        counts = [seg[base + (n_sub + s) * N_EXPERTS + ge0 + e] for s in range(n_sub)]
        offs = [pl.multiple_of(jnp.minimum(starts[s] + w * W, (s + 1) * MOE_CAP - W), MOE_ALIGN)
                for s in range(n_sub)]
        return e, w, offs, starts, counts

    def read_windows(it):
        out = []
        for j in range(MOE_ITEMS):
            e, w, offs, starts, counts = locate(it * MOE_ITEMS + j)
            out.append((e, jnp.concatenate(
                [buf_ref[pl.ds(offs[s], W), :] for s in range(n_sub)], axis=0)))
        return out

    def project_up(windows):
        for j, (e, xcat) in enumerate(windows):
            hid_ref[j, :, 0:n_f] = jnp.dot(xcat, w1_ref[e], preferred_element_type=F32)
            hid_ref[j, :, n_f:2 * n_f] = jnp.dot(xcat, w3_ref[e], preferred_element_type=F32)

    def finish(it, hidden):
        for j in range(MOE_ITEMS):
            e, w, offs, starts, counts = locate(it * MOE_ITEMS + j)
            hid = hidden[j]
            he = _silu(hid[:, 0:n_f]) * hid[:, n_f:2 * n_f]
            y = _dot(he, w2_ref[e])
            for s in range(n_sub):
                rid = (starts[s] + w * W + wrow).astype(F32)
                cols = slice(s * MOE_SUB, (s + 1) * MOE_SUB)
                wt = jnp.sum(jnp.where(pos1_row[:, cols] == rid, wt1_row[:, cols], 0.0)
                             + jnp.where(pos2_row[:, cols] == rid, wt2_row[:, cols], 0.0),
                             axis=-1, keepdims=True)
                valid = wrow < counts[s] - w * W
                new = (wt * y[s * W:(s + 1) * W]).astype(BF16)
                old = buf_ref[pl.ds(offs[s], W), :]
                buf_ref[pl.ds(offs[s], W), :] = jnp.where(valid, new, old)

    def pipeline(it, carry):
        windows = read_windows(it)
        hidden = [hid_ref[j] for j in range(MOE_ITEMS)]
        project_up(windows)
        finish(it - 1, hidden)
        return carry

    n_iter = (total + MOE_ITEMS - 1) // MOE_ITEMS

    @pl.when(total > 0)
    def _():
        project_up(read_windows(0))

    lax.fori_loop(1, n_iter, pipeline, 0)

    @pl.when(total > 0)
    def _():
        finish(n_iter - 1, [hid_ref[j] for j in range(MOE_ITEMS)])

    @pl.when(g == n_g - 1)
    def _():
        meta = meta_ref[0]
        slot_ids = lax.broadcasted_iota(jnp.int32, (1, MOE_CAP), 1).astype(F32)
        for s in range(n_sub):
            rows = slice(s * MOE_SUB, (s + 1) * MOE_SUB)
            p1 = meta[rows, 0:1] - float(s * MOE_CAP)
            p2 = meta[rows, 1:2] - float(s * MOE_CAP)
            hot = jnp.where((p1 == slot_ids) | (p2 == slot_ids), 1.0, 0.0).astype(BF16)
            moe = jnp.dot(hot, buf_ref[s * MOE_CAP:(s + 1) * MOE_CAP, :],
                          preferred_element_type=F32)
            x2 = x1_ref[0, rows, :] + gt2_ref[0] * moe
            ms = jnp.mean(x2 * x2, axis=-1, keepdims=True)
            y = (x2 * lax.rsqrt(ms + RMS_EPS)) * gf_ref[...]
            o_ref[0, rows, :] = y * (1.0 + scf_ref[0]) + shf_ref[0]


def _moe(seg, x1, h2, meta, metat, w1, w3, w2, gt2, g_final, shf, scf, *, tm):
    B, S, D = x1.shape
    E, _, F = w1.shape
    G = EXPERTS_PER_GROUP
    assert E == N_EXPERTS and MOE_CAP % MXU_TILE == 0 and tm % MOE_SUB == 0
    tok = lambda n: pl.BlockSpec((1, tm, n), lambda b, s, g, c: (b, s, 0))
    per_b = pl.BlockSpec((1, 1, D), lambda b, s, g, c: (b, 0, 0))
    grid_spec = pltpu.PrefetchScalarGridSpec(
        num_scalar_prefetch=1,
        grid=(B, S // tm, E // G),
        in_specs=[tok(D), tok(D), tok(128),
                  pl.BlockSpec((1, 8, tm), lambda b, s, g, c: (b, 0, s)),
                  pl.BlockSpec((G, D, F), lambda b, s, g, c: (g, 0, 0)),
                  pl.BlockSpec((G, D, F), lambda b, s, g, c: (g, 0, 0)),
                  pl.BlockSpec((G, F, D), lambda b, s, g, c: (g, 0, 0)),
                  per_b, pl.BlockSpec((1, D), lambda b, s, g, c: (0, 0)), per_b, per_b],
        out_specs=tok(D),
        scratch_shapes=[pltpu.VMEM(((tm // MOE_SUB) * MOE_CAP, D), BF16),
                        pltpu.VMEM((MOE_ITEMS, (tm // MOE_SUB) * MOE_WINDOW, 2 * F), F32)])
    return pl.pallas_call(
        _moe_kernel,
        grid_spec=grid_spec,
        out_shape=jax.ShapeDtypeStruct((B, S, D), F32),
        compiler_params=_cparams(3),
        name="moe",
    )(seg, x1, h2, meta, metat, w1, w3, w2, gt2.reshape(B, 1, D), g_final.reshape(1, D),
      shf.reshape(B, 1, D), scf.reshape(B, 1, D))


def _block_diag(w):
    G, n, _ = w.shape
    eye = jnp.eye(G, dtype=w.dtype)
    return (eye[:, None, :, None] * w[:, :, None, :]).reshape(G * n, G * n)


def kernel(x, c, w_ada, b_ada, g_mix, w_in, mu_rkv, mu_wag, w0, w1, w2, a0, a1, a2, g1, g2, k_k, k_a, r_k, lnx_g, lnx_b, conv_w, conv_b, w_rgate, b_rgate, w_igate, b_igate, lam, p_a, p_b, w_out, g_ffn, w_rg, b_rg, w_re, b_re, w1e, w3e, w2e, g_final, w_ada_f, b_ada_f):
    B, S, D = x.shape
    depth = w_ada.shape[0]
    assert depth == 1, "the final norm is fused into the last MoE call; one layer supported"
    A = w0.shape[-1]
    Bw = lam.shape[-1]

    c8 = jnp.zeros((8, D), F32).at[:B].set(c)
    modf = _ada(c8, w_ada_f, b_ada_f)[:B]
    shf, scf = modf[:, :D], modf[:, D:]

    head_ids = jnp.arange(QUAD * HEAD) // HEAD
    bd = (head_ids[:, None] == head_ids[None, :]).astype(BF16)

    for l in range(depth):
        mod = _ada(c8, w_ada[l], b_ada[l])[:B]
        sh1, sc1, gt1, sh2, sc2, gt2 = (mod[:, i * D:(i + 1) * D] for i in range(6))

        lowrank = jnp.concatenate([w1[l], a1[l], g1[l]], axis=1)
        mu_cols = jnp.concatenate([jnp.broadcast_to(mu_wag[l, 0][:, None], w1[l].shape),
                                   jnp.broadcast_to(mu_wag[l, 1][:, None], a1[l].shape),
                                   jnp.broadcast_to(mu_wag[l, 2][:, None], g1[l].shape)], axis=1)
        o1 = 3 * A
        wext = jnp.concatenate([w_in[l][:, :o1], mu_cols * lowrank, (1.0 - mu_cols) * lowrank,
                                w_in[l][:, o1:]], axis=1).astype(BF16)
        ra, rb = w2.shape[1], g2.shape[1]
        w2cat = jnp.zeros((2 * ra + rb, 3 * A), F32)
        w2cat = w2cat.at[0:ra, 0:A].set(w2[l]).at[ra:2 * ra, A:2 * A].set(a2[l])
        w2cat = w2cat.at[2 * ra:, 2 * A:].set(g2[l]).astype(BF16)

        per_tile = QUAD * HEAD // w_rgate.shape[-1]
        wide = lambda w: jax.vmap(_block_diag)(
            w.reshape(-1, per_tile, *w.shape[1:])).astype(BF16)
        (r, lw, k, v, kk, bvec, g, bonus, yb, s1, s2) = _inproj(
            x, sh1, sc1, g_mix[l], wext, mu_rkv[l], w0[l], a0[l], w2cat, k_k[l], k_a[l],
            r_k[l].reshape(-1), bd, conv_w[l].reshape(CONV_WIDTH, Bw), conv_b[l],
            wide(w_rgate[l]), b_rgate[l], wide(w_igate[l]), b_igate[l], lam[l])

        ya = _rwkv(r, lw, k, v, kk, bvec)

        n_g, n_e = w_rg.shape[-1], w_re.shape[-1]
        wr = jnp.zeros((D, 128), F32).at[:, :n_g].set(w_rg[l]).at[:, n_g:n_g + n_e].set(w_re[l])
        br = jnp.zeros((128,), F32).at[:n_g].set(b_rg[l]).at[n_g:n_g + n_e].set(b_re[l])
        wr = jnp.concatenate(_split(wr), axis=1)
        x1, h2, logits = _merge(
            x, ya, bonus, g, yb, s1, s2, gt1, sh2, sc2, lnx_g[l], lnx_b[l], bd,
            p_a[l].astype(BF16), p_b[l].astype(BF16), w_out[l].astype(BF16),
            g_ffn[l], wr, br)
        meta, metat, seg = _router(logits, tm=MOE_TILE)
        seg = seg[:, :, :, n_g:n_g + n_e].astype(jnp.int32).reshape(-1)
        x = _moe(seg, x1, h2, meta, metat, w1e[l].astype(BF16), w3e[l].astype(BF16),
                 w2e[l].astype(BF16), gt2, g_final, shf, scf, tm=MOE_TILE)
    return x
```

```python
import functools

import jax
import jax.numpy as jnp
from jax import lax
from jax.experimental import pallas as pl
from jax.experimental.pallas import tpu as pltpu

F32 = jnp.float32
BF16 = jnp.bfloat16

MXU_TILE = 256
HEAD = 64
CHUNK = 64
SUB = 16
QUAD = 4
RWKV_STEP_CHUNKS = 4
RMS_EPS = 1e-6
GN_EPS = 64e-5
LRU_C = 8.0
CONV_WIDTH = 4
N_GROUPS = 4
EXPERTS_PER_GROUP = 8
N_EXPERTS = N_GROUPS * EXPERTS_PER_GROUP
MOE_TILE = 1024
MOE_SUB = 256
MOE_ALIGN = 16
MOE_WINDOW = 32
MOE_ITEMS = 4
MOE_CAP = 2 * MOE_SUB + N_EXPERTS * (MOE_ALIGN - 1) + MOE_WINDOW
VMEM_LIMIT = 58 * 1024 * 1024


def _cparams(n_axes):
    return pltpu.CompilerParams(dimension_semantics=("arbitrary",) * n_axes,
                                vmem_limit_bytes=VMEM_LIMIT)


def _dot(a, b, dims=(((1,), (0,)), ((), ()))):
    return lax.dot_general(a.astype(BF16), b.astype(BF16), dims,
                           preferred_element_type=F32)


NT = (((1,), (1,)), ((), ()))
TN = (((0,), (0,)), ((), ()))


def _split(a):
    hi = a.astype(BF16)
    lo = (a - hi.astype(F32)).astype(BF16)
    return hi, lo


def _dot_hl(a, b_exact, dims=(((1,), (0,)), ((), ()))):
    hi, lo = _split(a)
    return (lax.dot_general(hi, b_exact, dims, preferred_element_type=F32)
            + lax.dot_general(lo, b_exact, dims, preferred_element_type=F32))


def _head_sums(a, ones_blocks):
    w = ones_blocks.shape[0]
    parts = [_dot_hl(a[:, o:o + w], ones_blocks) for o in range(0, a.shape[1], w)]
    return parts[0] if len(parts) == 1 else jnp.concatenate(parts, axis=1)


def _dot3(a, b, dims=(((1,), (0,)), ((), ()))):
    ah, al = _split(a)
    bh, bl = _split(b)
    d = functools.partial(lax.dot_general, dimension_numbers=dims,
                          preferred_element_type=F32)
    return d(ah, bh) + (d(ah, bl) + d(al, bh))


def _sigmoid(z):
    return 0.5 * jnp.tanh(0.5 * z) + 0.5


def _softplus(z):
    return jnp.maximum(z, 0.0) + jnp.log1p(jnp.exp(-jnp.abs(z)))


def _silu(z):
    return z * _sigmoid(z)


def _gelu_tanh(z):
    return 0.5 * z * (1.0 + jnp.tanh(0.7978845608028654 * (z + 0.044715 * (z * z * z))))


def _ada_kernel(c_ref, w_ref, b_ref, o_ref):
    ca = _silu(c_ref[...])
    o_ref[...] = _dot3(ca, w_ref[...]) + b_ref[...]


def _ada(c8, w, b, tn=1024):
    d, n = w.shape
    return pl.pallas_call(
        _ada_kernel,
        grid=(n // tn,),
        in_specs=[pl.BlockSpec((8, d), lambda j: (0, 0)),
                  pl.BlockSpec((d, tn), lambda j: (0, j)),
                  pl.BlockSpec((1, tn), lambda j: (0, j))],
        out_specs=pl.BlockSpec((8, tn), lambda j: (0, j)),
        out_shape=jax.ShapeDtypeStruct((8, n), F32),
        compiler_params=_cparams(1),
        name="adaln",
    )(c8, w, b.reshape(1, n))


def _inproj_kernel(x_ref, sh_ref, sc_ref, g_ref, wext_ref, mu_ref, w0_ref, a0_ref,
                   w2cat_ref, kk_ref, ka_ref, rk_ref, bd_ref,
                   cw_ref, cb_ref, wr_ref, br_ref, wi_ref, bi_ref, lam_ref,
                   r_out, lw_out, k_out, v_out, kk_out, b_out, g_out, bonus_out,
                   yb_out, s1_out, s2_out, carry_ref, xc_carry, h_carry, *, a_width, b_width,
                   d_model):
    A = a_width
    n_shift = 3 * A + 256
    s = pl.program_id(1)

    @pl.when(s == 0)
    def _():
        carry_ref[...] = jnp.zeros_like(carry_ref)
        xc_carry[...] = jnp.zeros_like(xc_carry)
        h_carry[...] = jnp.zeros_like(h_carry)

    x = x_ref[0]
    tm = x.shape[0]
    ms = jnp.mean(x * x, axis=-1, keepdims=True)
    h = (x * lax.rsqrt(ms + RMS_EPS)) * g_ref[...]
    h = h * (1.0 + sc_ref[0]) + sh_ref[0]
    hb = h.astype(BF16)

    sg = jnp.dot(hb, wext_ref[:, 0:n_shift + 256], preferred_element_type=F32)
    o = n_shift + 256
    xg = jnp.dot(hb, wext_ref[:, o:o + 2 * b_width], preferred_element_type=F32)
    o += 2 * b_width

    def gate_strand():
        step = 2 * MXU_TILE
        for out, base in ((s1_out, o), (s2_out, o + d_model)):
            for c in range(0, d_model, step):
                out[0, :, c:c + step] = _sigmoid(jnp.dot(
                    hb, wext_ref[:, base + c:base + c + step],
                    preferred_element_type=F32)).astype(out.dtype)
                yield

    def rwkv_strand():
        yield from _rwkv_heads(sg, n_shift, A, carry_ref, mu_ref, w0_ref, a0_ref, w2cat_ref,
                               kk_ref, ka_ref, rk_ref, bd_ref, r_out, lw_out, k_out, v_out,
                               kk_out, b_out, g_out, bonus_out)

    strands = [gate_strand(), rwkv_strand(),
               _rglru_tile(xg[:, 0:b_width], xg[:, b_width:2 * b_width], cw_ref, cb_ref, wr_ref,
                           br_ref, wi_ref, bi_ref, lam_ref, xc_carry, h_carry, yb_out)]
    while strands:
        strands = [c for c in strands if next(c, "done") != "done"]


def _rwkv_heads(sg, n_shift, A, carry_ref, mu_ref, w0_ref, a0_ref, w2cat_ref, kk_ref, ka_ref,
                rk_ref, bd_ref, r_out, lw_out, k_out, v_out, kk_out, b_out, g_out, bonus_out):
    tm = sg.shape[0]
    cur = sg[:, 0:n_shift]
    row = lax.broadcasted_iota(jnp.int32, (tm, 1), 0)
    prev = jnp.where(row == 0, carry_ref[0:1, :], pltpu.roll(cur, 1, axis=0))
    carry_ref[0:1, :] = cur[tm - 1:tm, :]

    rkv = cur[:, 0:3 * A]
    rkv = rkv + (prev[:, 0:3 * A] - rkv) * mu_ref[...]
    r = rkv[:, 0:A]
    k = rkv[:, A:2 * A]
    v = rkv[:, 2 * A:3 * A]
    r_out[0] = r
    v_out[0] = v
    yield

    pre = sg[:, n_shift:n_shift + 256] + prev[:, 3 * A:3 * A + 256]
    lane = lax.broadcasted_iota(jnp.int32, pre.shape, 1)
    act = jnp.where(lane < 64, jnp.tanh(pre), jnp.where(lane < 128, pre, _sigmoid(pre)))
    low = _dot(act, w2cat_ref[...])
    bd = bd_ref[...]
    kkr = k * kk_ref[...]
    kk_ss = _head_sums(kkr * kkr, bd)
    yield
    w_log = -_softplus(-(w0_ref[...] + low[:, 0:A])) - 0.5
    lw_out[0] = -jnp.exp(w_log)
    g_out[0] = low[:, 2 * A:3 * A].astype(g_out.dtype)
    yield
    iclr = _sigmoid(a0_ref[...] + low[:, A:2 * A])
    kkn = kkr * lax.rsqrt(kk_ss + 1e-12)
    k2 = k * (1.0 + (iclr - 1.0) * ka_ref[...])
    k_out[0] = k2
    kk_out[0] = kkn
    b_out[0] = kkn * iclr
    yield
    bonus_out[0] = (_head_sums(r * k2 * rk_ref[...], bd) * v).astype(bonus_out.dtype)


def _inproj(x, sh1, sc1, g_mix, wext, mu_rkv, w0, a0, w2cat, k_k, k_a, r_k, bd,
            conv_w, conv_b, wr, br, wi, bi, lam, *, tm=512):
    B, S, D = x.shape
    A = w0.shape[-1]
    Bw = lam.shape[-1]
    row = lambda a: a.reshape(1, -1)
    full = lambda a: pl.BlockSpec(a.shape, lambda b, s: (0,) * a.ndim,
                                  pipeline_mode=pl.Buffered(1))
    tok = lambda n: pl.BlockSpec((1, tm, n), lambda b, s: (b, s, 0))
    per_b = pl.BlockSpec((1, 1, D), lambda b, s: (b, 0, 0))
    small = [row(g_mix), wext, row(mu_rkv), row(w0), row(a0), w2cat, row(k_k), row(k_a),
             row(r_k), bd, conv_w, row(conv_b), wr, row(br), wi, row(bi), row(lam)]
    outs = [jax.ShapeDtypeStruct((B, S, A), F32)] * 6 + \
           [jax.ShapeDtypeStruct((B, S, A), BF16)] * 2 + \
           [jax.ShapeDtypeStruct((B, S, Bw), BF16)] + \
           [jax.ShapeDtypeStruct((B, S, D), BF16)] * 2
    return pl.pallas_call(
        functools.partial(_inproj_kernel, a_width=A, b_width=Bw, d_model=D),
        grid=(B, S // tm),
        in_specs=[tok(D), per_b, per_b] + [full(a) for a in small],
        out_specs=[tok(A)] * 8 + [tok(Bw)] + [tok(D)] * 2,
        out_shape=outs,
        scratch_shapes=[pltpu.VMEM((8, 3 * A + 256), F32), pltpu.VMEM((8, Bw), F32),
                        pltpu.VMEM((8, Bw), F32)],
        compiler_params=_cparams(2),
        name="inproj",
    )(x, sh1.reshape(B, 1, D), sc1.reshape(B, 1, D), *small)


def _rwkv_kernel(r_ref, lw_ref, k_ref, v_ref, kk_ref, b_ref, tri_ref, bdm_ref, slm_ref,
                 lm_ref, dgm_ref, bkm_ref, y_ref, s_ref):
    @pl.when(pl.program_id(0) == 0)
    def _():
        s_ref[...] = jnp.zeros_like(s_ref)

    W = QUAD * HEAD
    n_b, rows, n_a = r_ref.shape
    masks = (tri_ref[...], bdm_ref[...], slm_ref[...], lm_ref[...], dgm_ref[...], bkm_ref[...])
    chains = []
    state_version = {}
    for c in range(rows // CHUNK):
        t = slice(c * CHUNK, (c + 1) * CHUNK)
        for bi in range(n_b):
            for q in range(n_a // W):
                cols = slice(q * W, (q + 1) * W)
                idx = bi * (n_a // W) + q
                ins = [ref[bi, t, cols] for ref in (r_ref, lw_ref, k_ref, v_ref, kk_ref, b_ref)]
                chains.append(_rwkv_chunk(*ins, s_ref, idx, masks, state_version, c,
                                          functools.partial(_store_y, y_ref, bi, t, cols)))
    while chains:
        chains = [c for c in chains if next(c, "done") != "done"]


def _store_y(y_ref, bi, t, cols, y):
    y_ref[bi, t, cols] = y


def _rwkv_chunk(r, lw, k, v, kk, b, s_ref, idx, masks, state_version, version, store_y):
    tri, bdm, slm, lm, dgm, bkm = masks
    L = CHUNK
    a = -kk

    cl = _dot_hl3(tri, lw)
    clp = cl - lw
    cm = cl[L // 2 - 1:L // 2, :]
    ce = cl[L - 1:L, :]
    at = a * jnp.exp(clp - cm)
    rt = r * jnp.exp(cl - cm)
    e_inv = jnp.exp(cm - cl)
    bt = b * e_inv
    kt = k * e_inv
    e_end = jnp.exp(ce - cl)
    bh = b * e_end
    kh = k * e_end
    g_mid = jnp.exp(cm)
    g_end = jnp.exp(ce)

    tile = lambda m: jnp.concatenate([m] * QUAD, axis=0)
    stack = lambda m: tile(m) * bdm
    a_s = stack(at)
    r_s = stack(rt)
    v_s = stack(v)
    bh_s = stack(bh)
    kh_s = stack(kh)
    bt_t = tile(bt)
    kt_t = tile(kt)
    yield

    a_ab = _dot(a_s, bt_t, NT) * slm
    a_ak = _dot(a_s, kt_t, NT) * slm
    a_rb = _dot(r_s, bt_t, NT) * lm
    a_rk = _dot(r_s, kt_t, NT) * lm
    yield

    d = a_ab * dgm
    e = a_ab - d
    n_blk = d.shape[0] // SUB
    fold = lambda m: functools.reduce(
        lambda acc, b: acc + m[b * SUB:(b + 1) * SUB], range(1, n_blk), m[0:SUB])
    unfold = lambda strip: jnp.concatenate([strip] * n_blk, axis=0) * bkm
    p_full = d
    p_strip = fold(d)
    d_inv_strip = fold(lm - slm) + p_strip
    n = 2
    while n < SUB:
        p_strip = _dot(p_strip, p_full)
        yield
        p_full = unfold(p_strip)
        d_inv_strip = d_inv_strip + _dot(d_inv_strip, p_full)
        yield
        n *= 2
    d_inv = unfold(d_inv_strip)
    f = _dot(d_inv, e)
    yield
    t_inv = d_inv
    fp = f
    n = 1
    terms = []
    while n < L // SUB:
        terms.append(fp)
        n *= 2
        if n < L // SUB:
            fp = _dot(fp, fp)
            yield
    for fp in reversed(terms):
        t_inv = t_inv + _dot(fp, t_inv)
        yield

    while state_version.get(idx, 0) != version:
        yield
    s0 = s_ref[idx]
    s_mid = s0 * g_mid
    rhs = _dot(a_s, s_mid, NT) + _dot(a_ak, v_s)
    y_part = _dot(r_s, s_mid, NT) + _dot(a_rk, v_s)
    yield
    u = _dot(t_inv, rhs)
    yield
    y_s = y_part + _dot(a_rb, u)
    s_ref[idx] = s0 * g_end + _dot(u, bh_s, TN) + _dot(v_s, kh_s, TN)
    state_version[idx] = version + 1
    y = y_s[0:L]
    for q in range(1, QUAD):
        y = y + y_s[q * L:(q + 1) * L]
    store_y(y)


def _dot_hl3(tri, lw):
    hi = lw.astype(BF16)
    r1 = lw - hi.astype(F32)
    mid = r1.astype(BF16)
    lo = (r1 - mid.astype(F32)).astype(BF16)
    d = functools.partial(jnp.dot, preferred_element_type=F32)
    return d(tri, hi) + (d(tri, mid) + d(tri, lo))


def _rwkv(r, lw, k, v, kk, b):
    B, S, A = r.shape
    W = QUAD * HEAD
    L = CHUNK
    idx = jnp.arange(W)
    assert L == HEAD, "stacked rows (head, time) and lanes (head, channel) share one block size"
    same = (idx[:, None] // L) == (idx[None, :] // L)
    ti = idx[:, None] % L
    tj = idx[None, :] % L
    bdm = same.astype(F32)
    slm = (same & (ti > tj)).astype(F32)
    lm = (same & (ti >= tj)).astype(F32)
    bkm = (same & (ti // SUB == tj // SUB)).astype(F32)
    dgm = (same & (ti > tj) & (ti // SUB == tj // SUB)).astype(F32)
    tri = (jnp.arange(L)[:, None] >= jnp.arange(L)[None, :]).astype(BF16)
    rows = RWKV_STEP_CHUNKS * L
    tok = pl.BlockSpec((B, rows, A), lambda c: (0, c, 0))
    full = lambda a: pl.BlockSpec(a.shape, lambda c: (0,) * a.ndim)
    return pl.pallas_call(
        _rwkv_kernel,
        grid=(S // rows,),
        in_specs=[tok] * 6 + [full(a) for a in (tri, bdm, slm, lm, dgm, bkm)],
        out_specs=tok,
        out_shape=jax.ShapeDtypeStruct((B, S, A), F32),
        scratch_shapes=[pltpu.VMEM((B * (A // W), W, W), F32)],
        compiler_params=_cparams(1),
        name="rwkv7",
    )(r, lw, k, v, kk, b, tri, bdm, slm, lm, dgm, bkm)


def _wide_block_dot(a, w_ref):
    w = w_ref.shape[-1]
    parts = [_dot(a[:, i * w:(i + 1) * w], w_ref[i]) for i in range(w_ref.shape[0])]
    return parts[0] if len(parts) == 1 else jnp.concatenate(parts, axis=1)


def _rglru_tile(xb, gb, cw_ref, cb_ref, wr_ref, br_ref, wi_ref, bi_ref, lam_ref,
                xc_carry, h_carry, y_out):
    tt, width = xb.shape
    ext = jnp.concatenate([xc_carry[...], xb], axis=0)
    xc_carry[...] = xb[tt - 8:tt, :]
    xc = cb_ref[...] + cw_ref[CONV_WIDTH - 1:CONV_WIDTH, :] * xb
    for d in range(1, CONV_WIDTH):
        xc = xc + cw_ref[CONV_WIDTH - 1 - d:CONV_WIDTH - d, :] * ext[8 - d:8 - d + tt, :]

    pre_r = _wide_block_dot(xc, wr_ref)
    pre_i = _wide_block_dot(xc, wi_ref)
    yield
    gate_r = _sigmoid(pre_r + br_ref[...])
    gate_i = _sigmoid(pre_i + bi_ref[...])
    log_a = -LRU_C * gate_r * _softplus(-lam_ref[...])
    a = jnp.exp(log_a)
    u = jnp.sqrt(-jnp.tanh(log_a) * (a * a + 1.0)) * (gate_i * xc)
    yield

    a3 = a.reshape(tt // 8, 8, width)
    u3 = u.reshape(tt // 8, 8, width)
    sub = lax.broadcasted_iota(jnp.int32, (1, 8, 1), 1)
    d = 1
    while d < 8:
        keep = sub >= d
        a_sh = jnp.where(keep, pltpu.roll(a3, d, axis=1), 1.0)
        u_sh = jnp.where(keep, pltpu.roll(u3, d, axis=1), 0.0)
        u3 = a3 * u_sh + u3
        a3 = a3 * a_sh
        d *= 2
        yield
    gate = _gelu_tanh(gb)
    carry = h_carry[0:1, :]
    groups = []
    for i in range(tt // 8):
        hg = a3[i] * carry + u3[i]
        groups.append(hg)
        carry = hg[7:8, :]
        if i % 8 == 7:
            yield
    h_carry[0:1, :] = carry
    y_out[0] = (jnp.concatenate(groups, axis=0) * gate).astype(y_out.dtype)


def _merge_kernel(x_ref, ya_ref, bonus_ref, g_ref, yb_ref, s1_ref, s2_ref, gt1_ref, sh2_ref,
                  sc2_ref, lng_ref, lnb_ref, bd_ref, pa_ref, pb_ref, wout_ref, gffn_ref,
                  wr_ref, br_ref, x1_out, h2_out, logit_out):
    ya = ya_ref[0]
    bd = bd_ref[...]
    inv_n = 1.0 / HEAD
    mean = _head_sums(ya, bd) * inv_n
    yc = ya - mean
    var = _head_sums(yc * yc, bd) * inv_n
    yn = yc * lax.rsqrt(var + GN_EPS) * lng_ref[...] + lnb_ref[...]
    ya2 = (yn + bonus_ref[0]) * g_ref[0]

    merged = s1_ref[0] * _dot(ya2, pa_ref[...]) + s2_ref[0] * _dot(yb_ref[0], pb_ref[...])
    x1 = x_ref[0] + gt1_ref[0] * _dot(merged, wout_ref[...])
    x1_out[0] = x1

    ms = jnp.mean(x1 * x1, axis=-1, keepdims=True)
    h2 = (x1 * lax.rsqrt(ms + RMS_EPS)) * gffn_ref[...]
    h2 = h2 * (1.0 + sc2_ref[0]) + sh2_ref[0]
    h2_out[0] = h2.astype(BF16)

    n_log = br_ref.shape[1]
    h2_hi, h2_lo = _split(h2)
    parts = jnp.dot(jnp.concatenate([h2_hi, h2_lo], axis=0), wr_ref[...],
                    preferred_element_type=F32)
    tm = h2.shape[0]
    logit_out[0] = ((parts[:tm, :n_log] + parts[:tm, n_log:])
                    + (parts[tm:, :n_log] + parts[tm:, n_log:])) + br_ref[...]


def _router_kernel(logit_ref, tri_ref, utri_ref, meta_out, metat_out, seg_out):
    logits = logit_ref[0]
    lane = lax.broadcasted_iota(jnp.int32, logits.shape, 1).astype(F32)
    neg = jnp.float32(-jnp.inf)
    big = jnp.float32(1e9)
    is_g = lane < N_GROUPS
    lg = jnp.where(is_g, logits, neg)
    mg = jnp.max(lg, axis=-1, keepdims=True)
    pg_top = 1.0 / jnp.sum(jnp.exp(lg - mg), axis=-1, keepdims=True)
    g_idx = jnp.min(jnp.where(lg == mg, lane, big), axis=-1, keepdims=True)
    e_lane = lane - N_GROUPS
    in_grp = (e_lane >= g_idx * EXPERTS_PER_GROUP) & (e_lane < (g_idx + 1) * EXPERTS_PER_GROUP)
    le = jnp.where(in_grp, logits, neg)
    me = jnp.max(le, axis=-1, keepdims=True)
    i1 = jnp.min(jnp.where(le == me, lane, big), axis=-1, keepdims=True)
    ee = jnp.exp(le - me)
    se = jnp.sum(ee, axis=-1, keepdims=True)
    rest = jnp.where(lane != i1, le, neg)
    m2 = jnp.max(rest, axis=-1, keepdims=True)
    i2 = jnp.min(jnp.where(rest == m2, lane, big), axis=-1, keepdims=True)
    p1 = 1.0 / se
    p2 = jnp.exp(m2 - me) / se
    den = p1 + p2
    wt1 = pg_top * (p1 / den)
    wt2 = pg_top * (p2 / den)

    hot1 = jnp.where(lane == i1, 1.0, 0.0)
    hot2 = jnp.where(lane == i2, 1.0, 0.0)
    both = hot1 + hot2
    n_sub = logits.shape[0] // MOE_SUB
    pos1, pos2, seg_rows = [], [], []
    for s in range(n_sub):
        rows = slice(s * MOE_SUB, (s + 1) * MOE_SUB)
        before = _dot(tri_ref[...], both[rows])
        count = jnp.sum(both[rows], axis=0, keepdims=True)
        padded = jnp.floor((count + (MOE_ALIGN - 1)) * (1.0 / MOE_ALIGN)) * MOE_ALIGN
        start = _dot_hl(jnp.broadcast_to(padded, (8, padded.shape[1])), utri_ref[...])[0:1] \
            + float(s * MOE_CAP)
        slot = before + start
        pos1.append(jnp.sum(hot1[rows] * slot, axis=-1, keepdims=True))
        pos2.append(jnp.sum(hot2[rows] * slot, axis=-1, keepdims=True))
        seg_rows.append((start, count))
    pos1 = jnp.concatenate(pos1, axis=0)
    pos2 = jnp.concatenate(pos2, axis=0)
    meta = jnp.where(lane == 0, pos1, jnp.where(lane == 1, pos2,
                     jnp.where(lane == 2, wt1, jnp.where(lane == 3, wt2, 0.0))))
    meta_out[0] = meta
    metat_out[0] = jnp.transpose(meta)[0:8, :]
    seg = [st for st, _ in seg_rows] + [ct for _, ct in seg_rows]
    seg_out[0, 0] = jnp.concatenate(seg, axis=0)


def _router(logits, *, tm):
    B, S, n = logits.shape
    assert 2 * (tm // MOE_SUB) == 8, "segment table holds one start and one count row per sub-tile"
    tri = (jnp.arange(MOE_SUB)[:, None] > jnp.arange(MOE_SUB)[None, :]).astype(BF16)
    utri = (jnp.arange(n)[:, None] < jnp.arange(n)[None, :]).astype(BF16)
    tok = pl.BlockSpec((1, tm, n), lambda b, s: (b, s, 0))
    full = lambda a: pl.BlockSpec(a.shape, lambda b, s: (0,) * a.ndim)
    return pl.pallas_call(
        _router_kernel,
        grid=(B, S // tm),
        in_specs=[tok, full(tri), full(utri)],
        out_specs=[tok, pl.BlockSpec((1, 8, tm), lambda b, s: (b, 0, s)),
                   pl.BlockSpec((1, 1, 8, n), lambda b, s: (b, s, 0, 0))],
        out_shape=[jax.ShapeDtypeStruct((B, S, n), F32),
                   jax.ShapeDtypeStruct((B, 8, S), F32),
                   jax.ShapeDtypeStruct((B, S // tm, 8, n), F32)],
        compiler_params=_cparams(2),
        name="router",
    )(logits, tri, utri)


def _merge(x, ya, bonus, g, yb, s1, s2, gt1, sh2, sc2, lnx_g, lnx_b, bd, pa, pb, wout, g_ffn,
           wr, br, *, tm=512):
    B, S, D = x.shape
    A = ya.shape[-1]
    n = br.shape[-1]
    row = lambda a: a.reshape(1, -1)
    full = lambda a: pl.BlockSpec(a.shape, lambda b, s: (0,) * a.ndim)
    tok = lambda n: pl.BlockSpec((1, tm, n), lambda b, s: (b, s, 0))
    per_b = pl.BlockSpec((1, 1, D), lambda b, s: (b, 0, 0))
    small = [row(lnx_g), row(lnx_b), bd, pa, pb, wout, row(g_ffn), wr, row(br)]
    return pl.pallas_call(
        _merge_kernel,
        grid=(B, S // tm),
        in_specs=[tok(D), tok(A), tok(A), tok(A), tok(A), tok(D), tok(D), per_b, per_b, per_b]
        + [full(a) for a in small],
        out_specs=[tok(D), tok(D), tok(n)],
        out_shape=[jax.ShapeDtypeStruct((B, S, D), F32),
                   jax.ShapeDtypeStruct((B, S, D), BF16),
                   jax.ShapeDtypeStruct((B, S, n), F32)],
        compiler_params=_cparams(2),
        name="merge",
    )(x, ya, bonus, g, yb, s1, s2, gt1.reshape(B, 1, D), sh2.reshape(B, 1, D),
      sc2.reshape(B, 1, D), *small)


def _moe_kernel(seg, x1_ref, h2_ref, meta_ref, metat_ref, w1_ref, w3_ref, w2_ref,
                gt2_ref, gf_ref, shf_ref, scf_ref, o_ref, buf_ref, hid_ref):
    b = pl.program_id(0)
    i = pl.program_id(1)
    g = pl.program_id(2)
    n_g = pl.num_programs(2)
    tm = h2_ref.shape[1]
    n_sub = tm // MOE_SUB
    W = MOE_WINDOW
    base = (b * pl.num_programs(1) + i) * (2 * n_sub * N_EXPERTS)

    def local(slots, s):
        return slots[:, s * MOE_SUB:(s + 1) * MOE_SUB] - float(s * MOE_CAP)

    pos1_row = metat_ref[0, 0:1, :]
    pos2_row = metat_ref[0, 1:2, :]
    wt1_row = metat_ref[0, 2:3, :]
    wt2_row = metat_ref[0, 3:4, :]

    @pl.when(g == 0)
    def _():
        blk = MXU_TILE
        for s in range(n_sub):
            p1 = local(pos1_row, s)
            p2 = local(pos2_row, s)
            h2s = h2_ref[0, s * MOE_SUB:(s + 1) * MOE_SUB, :]
            for r0 in range(0, MOE_CAP, blk):
                rid = lax.broadcasted_iota(jnp.int32, (blk, 1), 0).astype(F32) + float(r0)
                hot = jnp.where((p1 == rid) | (p2 == rid), 1.0, 0.0).astype(BF16)
                buf_ref[s * MOE_CAP + r0:s * MOE_CAP + r0 + blk, :] = jnp.dot(
                    hot, h2s, preferred_element_type=F32).astype(BF16)

    wrow = lax.broadcasted_iota(jnp.int32, (W, 1), 0)
    ge0 = g * EXPERTS_PER_GROUP
    first = [jnp.int32(0)]
    for e in range(EXPERTS_PER_GROUP):
        n_max = functools.reduce(
            jnp.maximum, [seg[base + (n_sub + s) * N_EXPERTS + ge0 + e] for s in range(n_sub)])
        first.append(first[-1] + (n_max + W - 1) // W)
    total = first[-1]
    n_f = w1_ref.shape[-1]

    def locate(k):
        inside = jnp.logical_and(k >= 0, k < total)
        e = functools.reduce(lambda a, c: a + (k >= c).astype(jnp.int32),
                             first[1:EXPERTS_PER_GROUP], jnp.int32(0))
        w = k - functools.reduce(lambda a, ec: jnp.where(e == ec[0], ec[1], a),
                                 enumerate(first[:-1]), jnp.int32(0))
        e = jnp.where(inside, e, EXPERTS_PER_GROUP - 1)
        w = jnp.where(inside, w, MOE_CAP // W)
        starts = [seg[base + s * N_EXPERTS + ge0 + e] for s in range(n_sub)]
        counts = [seg[base + (n_sub + s) * N_EXPERTS + ge0 + e] for s in range(n_sub)]
        offs = [pl.multiple_of(jnp.minimum(starts[s] + w * W, (s + 1) * MOE_CAP - W), MOE_ALIGN)
                for s in range(n_sub)]
        return e, w, offs, starts, counts

    def read_windows(it):
        out = []
        for j in range(MOE_ITEMS):
            e, w, offs, starts, counts = locate(it * MOE_ITEMS + j)
            out.append((e, jnp.concatenate(
                [buf_ref[pl.ds(offs[s], W), :] for s in range(n_sub)], axis=0)))
        return out

    def project_up(windows):
        for j, (e, xcat) in enumerate(windows):
            hid_ref[j, :, 0:n_f] = jnp.dot(xcat, w1_ref[e], preferred_element_type=F32)
            hid_ref[j, :, n_f:2 * n_f] = jnp.dot(xcat, w3_ref[e], preferred_element_type=F32)

    def finish(it, hidden):
        for j in range(MOE_ITEMS):
            e, w, offs, starts, counts = locate(it * MOE_ITEMS + j)
            hid = hidden[j]
            he = _silu(hid[:, 0:n_f]) * hid[:, n_f:2 * n_f]
            y = _dot(he, w2_ref[e])
            for s in range(n_sub):
                rid = (starts[s] + w * W + wrow).astype(F32)
                cols = slice(s * MOE_SUB, (s + 1) * MOE_SUB)
                wt = jnp.sum(jnp.where(pos1_row[:, cols] == rid, wt1_row[:, cols], 0.0)
                             + jnp.where(pos2_row[:, cols] == rid, wt2_row[:, cols], 0.0),
                             axis=-1, keepdims=True)
                valid = wrow < counts[s] - w * W
                new = (wt * y[s * W:(s + 1) * W]).astype(BF16)
                old = buf_ref[pl.ds(offs[s], W), :]
                buf_ref[pl.ds(offs[s], W), :] = jnp.where(valid, new, old)

    def pipeline(it, carry):
        windows = read_windows(it)
        hidden = [hid_ref[j] for j in range(MOE_ITEMS)]
        project_up(windows)
        finish(it - 1, hidden)
        return carry

    n_iter = (total + MOE_ITEMS - 1) // MOE_ITEMS

    @pl.when(total > 0)
    def _():
        project_up(read_windows(0))

    lax.fori_loop(1, n_iter, pipeline, 0)

    @pl.when(total > 0)
    def _():
        finish(n_iter - 1, [hid_ref[j] for j in range(MOE_ITEMS)])

    @pl.when(g == n_g - 1)
    def _():
        meta = meta_ref[0]
        slot_ids = lax.broadcasted_iota(jnp.int32, (1, MOE_CAP), 1).astype(F32)
        for s in range(n_sub):
            rows = slice(s * MOE_SUB, (s + 1) * MOE_SUB)
            p1 = meta[rows, 0:1] - float(s * MOE_CAP)
            p2 = meta[rows, 1:2] - float(s * MOE_CAP)
            hot = jnp.where((p1 == slot_ids) | (p2 == slot_ids), 1.0, 0.0).astype(BF16)
            moe = jnp.dot(hot, buf_ref[s * MOE_CAP:(s + 1) * MOE_CAP, :],
                          preferred_element_type=F32)
            x2 = x1_ref[0, rows, :] + gt2_ref[0] * moe
            ms = jnp.mean(x2 * x2, axis=-1, keepdims=True)
            y = (x2 * lax.rsqrt(ms + RMS_EPS)) * gf_ref[...]
            o_ref[0, rows, :] = y * (1.0 + scf_ref[0]) + shf_ref[0]


def _moe(seg, x1, h2, meta, metat, w1, w3, w2, gt2, g_final, shf, scf, *, tm):
    B, S, D = x1.shape
    E, _, F = w1.shape
    G = EXPERTS_PER_GROUP
    assert E == N_EXPERTS and MOE_CAP % MXU_TILE == 0 and tm % MOE_SUB == 0
    tok = lambda n: pl.BlockSpec((1, tm, n), lambda b, s, g, c: (b, s, 0))
    per_b = pl.BlockSpec((1, 1, D), lambda b, s, g, c: (b, 0, 0))
    grid_spec = pltpu.PrefetchScalarGridSpec(
        num_scalar_prefetch=1,
        grid=(B, S // tm, E // G),
        in_specs=[tok(D), tok(D), tok(128),
                  pl.BlockSpec((1, 8, tm), lambda b, s, g, c: (b, 0, s)),
                  pl.BlockSpec((G, D, F), lambda b, s, g, c: (g, 0, 0)),
                  pl.BlockSpec((G, D, F), lambda b, s, g, c: (g, 0, 0)),
                  pl.BlockSpec((G, F, D), lambda b, s, g, c: (g, 0, 0)),
                  per_b, pl.BlockSpec((1, D), lambda b, s, g, c: (0, 0)), per_b, per_b],
        out_specs=tok(D),
        scratch_shapes=[pltpu.VMEM(((tm // MOE_SUB) * MOE_CAP, D), BF16),
                        pltpu.VMEM((MOE_ITEMS, (tm // MOE_SUB) * MOE_WINDOW, 2 * F), F32)])
    return pl.pallas_call(
        _moe_kernel,
        grid_spec=grid_spec,
        out_shape=jax.ShapeDtypeStruct((B, S, D), F32),
        compiler_params=_cparams(3),
        name="moe",
    )(seg, x1, h2, meta, metat, w1, w3, w2, gt2.reshape(B, 1, D), g_final.reshape(1, D),
      shf.reshape(B, 1, D), scf.reshape(B, 1, D))


def _block_diag(w):
    G, n, _ = w.shape
    eye = jnp.eye(G, dtype=w.dtype)
    return (eye[:, None, :, None] * w[:, :, None, :]).reshape(G * n, G * n)


def kernel(x, c, w_ada, b_ada, g_mix, w_in, mu_rkv, mu_wag, w0, w1, w2, a0, a1, a2, g1, g2, k_k, k_a, r_k, lnx_g, lnx_b, conv_w, conv_b, w_rgate, b_rgate, w_igate, b_igate, lam, p_a, p_b, w_out, g_ffn, w_rg, b_rg, w_re, b_re, w1e, w3e, w2e, g_final, w_ada_f, b_ada_f):
    B, S, D = x.shape
    depth = w_ada.shape[0]
    assert depth == 1, "the final norm is fused into the last MoE call; one layer supported"
    A = w0.shape[-1]
    Bw = lam.shape[-1]

    c8 = jnp.zeros((8, D), F32).at[:B].set(c)
    modf = _ada(c8, w_ada_f, b_ada_f)[:B]
    shf, scf = modf[:, :D], modf[:, D:]

    head_ids = jnp.arange(QUAD * HEAD) // HEAD
    bd = (head_ids[:, None] == head_ids[None, :]).astype(BF16)

    for l in range(depth):
        mod = _ada(c8, w_ada[l], b_ada[l])[:B]
        sh1, sc1, gt1, sh2, sc2, gt2 = (mod[:, i * D:(i + 1) * D] for i in range(6))

        lowrank = jnp.concatenate([w1[l], a1[l], g1[l]], axis=1)
        mu_cols = jnp.concatenate([jnp.broadcast_to(mu_wag[l, 0][:, None], w1[l].shape),
                                   jnp.broadcast_to(mu_wag[l, 1][:, None], a1[l].shape),
                                   jnp.broadcast_to(mu_wag[l, 2][:, None], g1[l].shape)], axis=1)
        o1 = 3 * A
        wext = jnp.concatenate([w_in[l][:, :o1], mu_cols * lowrank, (1.0 - mu_cols) * lowrank,
                                w_in[l][:, o1:]], axis=1).astype(BF16)
        ra, rb = w2.shape[1], g2.shape[1]
        w2cat = jnp.zeros((2 * ra + rb, 3 * A), F32)
        w2cat = w2cat.at[0:ra, 0:A].set(w2[l]).at[ra:2 * ra, A:2 * A].set(a2[l])
        w2cat = w2cat.at[2 * ra:, 2 * A:].set(g2[l]).astype(BF16)

        per_tile = QUAD * HEAD // w_rgate.shape[-1]
        wide = lambda w: jax.vmap(_block_diag)(
            w.reshape(-1, per_tile, *w.shape[1:])).astype(BF16)
        (r, lw, k, v, kk, bvec, g, bonus, yb, s1, s2) = _inproj(
            x, sh1, sc1, g_mix[l], wext, mu_rkv[l], w0[l], a0[l], w2cat, k_k[l], k_a[l],
            r_k[l].reshape(-1), bd, conv_w[l].reshape(CONV_WIDTH, Bw), conv_b[l],
            wide(w_rgate[l]), b_rgate[l], wide(w_igate[l]), b_igate[l], lam[l])

        ya = _rwkv(r, lw, k, v, kk, bvec)

        n_g, n_e = w_rg.shape[-1], w_re.shape[-1]
        wr = jnp.zeros((D, 128), F32).at[:, :n_g].set(w_rg[l]).at[:, n_g:n_g + n_e].set(w_re[l])
        br = jnp.zeros((128,), F32).at[:n_g].set(b_rg[l]).at[n_g:n_g + n_e].set(b_re[l])
        wr = jnp.concatenate(_split(wr), axis=1)
        x1, h2, logits = _merge(
            x, ya, bonus, g, yb, s1, s2, gt1, sh2, sc2, lnx_g[l], lnx_b[l], bd,
            p_a[l].astype(BF16), p_b[l].astype(BF16), w_out[l].astype(BF16),
            g_ffn[l], wr, br)
        meta, metat, seg = _router(logits, tm=MOE_TILE)
        seg = seg[:, :, :, n_g:n_g + n_e].astype(jnp.int32).reshape(-1)
        x = _moe(seg, x1, h2, meta, metat, w1e[l].astype(BF16), w3e[l].astype(BF16),
                 w2e[l].astype(BF16), gt2, g_final, shf, scf, tm=MOE_TILE)
    return x
```
